```python
import jax, jax.numpy as jnp
from jax import lax
import numpy as np

D_MODEL = 1024
BATCH = 8
SEQ = 4096
DEPTH = 2
DEC_BATCH = 128
DEC_SEQ = 8
PAST_LEN = 16384
PAGE_SIZE = 128

HEAD_DIM = 64
ROPE_THETA = 10000.0
NORM_EPS = 1e-5
D_FF = 2816
CONV_DIM = D_MODEL // 2
CONV_WIDTH = 3
SWA_HEADS = (D_MODEL // 2) // HEAD_DIM
SWA_KV_HEADS = SWA_HEADS // 4
SWA_WINDOW = 128
MOBA_HEADS = (D_MODEL // 2) // HEAD_DIM
MOBA_KV_HEADS = MOBA_HEADS // 4
MOBA_BLOCK = 256
MOBA_TOPK = 3
NSA_HEADS = (D_MODEL // 2) // HEAD_DIM
NSA_KV_HEADS = NSA_HEADS // 4
NSA_CMP_LEN = 32
NSA_CMP_STRIDE = 16
NSA_CMP_HIDDEN = 64
NSA_SEL_BLOCK = 64
NSA_TOPN = 16
NSA_WINDOW = 512
D_IN_AB = 3 * CONV_DIM + (SWA_HEADS + 2 * SWA_KV_HEADS) * HEAD_DIM
D_IN_CD = (MOBA_HEADS + 2 * MOBA_KV_HEADS) * HEAD_DIM + (NSA_HEADS + 6 * NSA_KV_HEADS) * HEAD_DIM + 3 * NSA_HEADS
D_MIX_AB = CONV_DIM + SWA_HEADS * HEAD_DIM
D_MIX_CD = (MOBA_HEADS + NSA_HEADS) * HEAD_DIM
NEG_INF = -1e30
FORCE_SCORE = 1e9
QUERY_ROWS_PER_STEP = 128

kernel_name = 'hybrid_conv_swa_moba_nsa_macaron_step'


def _rmsnorm(x, g):
    xf = x.astype(jnp.float32)
    y = xf * lax.rsqrt(jnp.mean(xf * xf, axis=-1, keepdims=True) + NORM_EPS)
    return (y * g.astype(jnp.float32)).astype(x.dtype)


def _swiglu(x, w_gate, w_up, w_down):
    return (jax.nn.silu(x @ w_gate) * (x @ w_up)) @ w_down


def _split(z, sizes):
    out, start = [], 0
    for s in sizes:
        out.append(z[..., start:start + s])
        start += s
    return out


def _rope(x, pos):
    half = HEAD_DIM // 2
    inv_freq = ROPE_THETA ** (-jnp.arange(half, dtype=jnp.float32) * 2.0 / HEAD_DIM)
    ang = pos.astype(jnp.float32)[:, None] * inv_freq[None, :]
    cos = jnp.cos(ang)[None, :, None, :]
    sin = jnp.sin(ang)[None, :, None, :]
    xf = x.astype(jnp.float32)
    x1, x2 = xf[..., :half], xf[..., half:]
    return jnp.concatenate([x1 * cos - x2 * sin, x2 * cos + x1 * sin], -1).astype(x.dtype)


def _short_conv(u, w, buf):
    B, T, C = u.shape
    prev = jnp.zeros((B, CONV_WIDTH - 1, C), u.dtype) if buf is None else buf.astype(u.dtype)
    full = jnp.concatenate([prev, u], axis=1)
    y = lax.conv_general_dilated(full, w[:, None, :].astype(u.dtype), window_strides=(1,), padding='VALID',
                                 dimension_numbers=('NWC', 'WIO', 'NWC'), feature_group_count=C)
    return y, full[:, -(CONV_WIDTH - 1):]


def _window_attention(q, k_new, v_new, k_buf, v_buf, window, sinks=None):
    B, T, H, hd = q.shape
    G = k_new.shape[2]
    R = H // G
    n_buf = 0 if k_buf is None else k_buf.shape[1]

    def with_past(buf, new):
        parts = [jnp.zeros((B, window - n_buf, G, hd), new.dtype)]
        if buf is not None:
            parts.append(buf.astype(new.dtype))
        parts.append(new)
        return jnp.concatenate(parts, axis=1)

    k_all = with_past(k_buf, k_new)
    v_all = with_past(v_buf, v_new)
    qb = 128 if T % 128 == 0 else T
    nqb = T // qb
    span = qb + window
    idx = jnp.arange(nqb)[:, None] * qb + jnp.arange(span)[None, :]
    kb = k_all[:, idx]
    vb = v_all[:, idx]
    qr = q.reshape(B, nqb, qb, G, R, hd)
    s = jnp.einsum('bnqgrd,bnkgd->bngrqk', qr, kb).astype(jnp.float32) * (hd ** -0.5)
    q_rel = jnp.arange(nqb)[:, None] * qb + jnp.arange(qb)[None, :]
    k_rel = idx - window
    mask = ((k_rel[:, None, :] >= q_rel[:, :, None] - window) & (k_rel[:, None, :] <= q_rel[:, :, None])
            & (k_rel[:, None, :] >= -n_buf))[None, :, None, None]
    s = jnp.where(mask, s, NEG_INF)
    if sinks is None:
        p = jax.nn.softmax(s, axis=-1)
    else:
        sink = sinks.astype(jnp.float32).reshape(G, R)[None, None, :, :, None, None]
        m = jnp.maximum(jnp.max(s, axis=-1, keepdims=True), sink)
        e = jnp.exp(s - m)
        p = e / (jnp.sum(e, axis=-1, keepdims=True) + jnp.exp(sink - m))
    o = jnp.einsum('bngrqk,bnkgd->bnqgrd', p.astype(vb.dtype), vb)
    return o.reshape(B, T, H, hd)


def _window_tail(buf, new, window):
    if buf is None:
        return new[:, -min(window, new.shape[1]):]
    return jnp.concatenate([buf.astype(new.dtype), new], axis=1)[:, -buf.shape[1]:]


def _fetch(new, past, pos, head):
    B, T = new.shape[:2]
    bidx = jnp.arange(B).reshape((B,) + (1,) * (pos.ndim - 1))
    if past is None:
        return new[bidx, jnp.clip(pos, 0, T - 1), head]
    pool, table = past
    page_sz = pool.shape[1]
    n_past = table.shape[1] * page_sz
    pp = jnp.clip(pos, 0, n_past - 1)
    from_past = pool[table[bidx, pp // page_sz], pp % page_sz, head].astype(new.dtype)
    from_new = new[bidx, jnp.clip(pos - n_past, 0, T - 1), head]
    return jnp.where((pos < n_past)[..., None], from_past, from_new)


def _query_chunk(B, T):
    c = max(1, min(T, QUERY_ROWS_PER_STEP // B))
    while T % c:
        c -= 1
    return c


def _block_gather_attention(q, blocks, blk_valid, q_pos, blk_len, k_new, v_new, past_k, past_v):
    B, T, G, R, hd = q.shape
    S, nb = blocks.shape[3], blocks.shape[4]
    c = _query_chunk(B, T)
    n = T // c
    head = jnp.arange(G).reshape(1, 1, G, 1, 1)
    offs = jnp.arange(blk_len)

    def to_chunks(a):
        return jnp.moveaxis(a.reshape((B, n, c) + a.shape[2:]), 1, 0)

    def one(args):
        qc, bc, vc, pc = args
        kpos = (bc[..., None] * blk_len + offs).reshape(B, c, G, S, nb * blk_len)
        kmask = jnp.repeat(vc, blk_len, axis=-1) & (kpos <= pc[None, :, None, None, None])
        kr = _fetch(k_new, past_k, kpos, head)
        vr = _fetch(v_new, past_v, kpos, head)
        if S == 1:
            s = jnp.einsum('bcgrd,bcgkd->bcgrk', qc, kr[:, :, :, 0])
        else:
            s = jnp.einsum('bcgrd,bcgrkd->bcgrk', qc, kr)
        s = jnp.where(kmask, s.astype(jnp.float32) * (hd ** -0.5), NEG_INF)
        p = jax.nn.softmax(s, axis=-1).astype(vr.dtype)
        if S == 1:
            return jnp.einsum('bcgrk,bcgkd->bcgrd', p, vr[:, :, :, 0])
        return jnp.einsum('bcgrk,bcgrkd->bcgrd', p, vr)

    out = lax.map(one, (to_chunks(q), to_chunks(blocks), to_chunks(blk_valid), q_pos.reshape(n, c)))
    return jnp.moveaxis(out, 0, 1).reshape(B, T, G, R, hd)


def _page_sums(new, past):
    B, T, G, hd = new.shape
    parts = []
    if past is not None:
        pool, table = past
        parts.append(pool[table].astype(jnp.float32).sum(2))
    padded = jnp.pad(new, ((0, 0), (0, (-T) % PAGE_SIZE), (0, 0), (0, 0)))
    parts.append(padded.reshape(B, -1, PAGE_SIZE, G, hd).astype(jnp.float32).sum(2))
    return jnp.concatenate(parts, axis=1)


def _moba(q, k_new, v_new, past_k, past_v, pos):
    B, T, H, hd = q.shape
    G = k_new.shape[2]
    R = H // G
    qg = q.reshape(B, T, G, R, hd)
    sums = _page_sums(k_new, past_k)
    per = MOBA_BLOCK // PAGE_SIZE
    sums = jnp.pad(sums, ((0, 0), (0, (-sums.shape[1]) % per), (0, 0), (0, 0)))
    means = sums.reshape(B, -1, per, G, hd).sum(2) / MOBA_BLOCK
    n_blk = means.shape[1]
    cur = (pos // MOBA_BLOCK)[None, :, None, None, None]
    s = jnp.einsum('btgrd,bngd->btgrn', qg.astype(jnp.float32), means)
    s = jnp.where(jnp.arange(n_blk) < cur, s, -jnp.inf)
    _, idx = lax.top_k(s, min(MOBA_TOPK, n_blk))
    own = jnp.broadcast_to(cur, (B, T, G, R, 1)).astype(idx.dtype)
    blocks = jnp.concatenate([idx, own], axis=-1)
    valid = jnp.concatenate([idx < cur, jnp.ones((B, T, G, R, 1), bool)], axis=-1)
    return _block_gather_attention(qg, blocks, valid, pos, MOBA_BLOCK, k_new, v_new, past_k, past_v)


def _nsa_compress(rows_new, past, w1, b1, w2):
    B, T, G, hd = rows_new.shape
    w1h = w1.reshape(NSA_CMP_LEN // NSA_CMP_STRIDE, NSA_CMP_STRIDE, hd, NSA_CMP_HIDDEN)

    def chunk_proj(rows):
        r = rows.reshape(B, -1, NSA_CMP_STRIDE, G, hd)
        return jnp.einsum('bnsgd,hsdk->bnhgk', r, w1h)

    parts = []
    if past is not None:
        pool, table = past
        parts.append(chunk_proj(pool[table].reshape(B, -1, G, hd).astype(rows_new.dtype)))
    parts.append(chunk_proj(jnp.pad(rows_new, ((0, 0), (0, (-T) % NSA_CMP_STRIDE), (0, 0), (0, 0)))))
    hp = jnp.concatenate(parts, axis=1)
    hid = jax.nn.gelu(hp[:, :-1, 0] + hp[:, 1:, 1] + b1)
    return jnp.einsum('bngk,kd->bngd', hid, w2)


def _nsa(q, kc, vc, ks, vs, kw, vw, gates, pos, p0, cmp_k_params, cmp_v_params,
         past_ck, past_cv, past_sk, past_sv, win_k_buf, win_v_buf):
    B, T, H, hd = q.shape
    G = ks.shape[2]
    R = H // G
    L = p0 + T
    qg = q.reshape(B, T, G, R, hd)
    ck = _nsa_compress(kc, past_ck, *cmp_k_params)
    cv = _nsa_compress(vc, past_cv, *cmp_v_params)
    n_cmp = ck.shape[1]
    c_end = jnp.arange(n_cmp) * NSA_CMP_STRIDE + (NSA_CMP_LEN - 1)
    c_mask = (c_end[None, :] <= pos[:, None])[None, :, None, None, :]
    s = jnp.einsum('btgrd,bngd->btgrn', qg, ck).astype(jnp.float32) * (hd ** -0.5)
    p = jax.nn.softmax(jnp.where(c_mask, s, NEG_INF), axis=-1) * c_mask
    o_cmp = jnp.einsum('btgrn,bngd->btgrd', p.astype(cv.dtype), cv)
    imp = jnp.sum(p, axis=3)
    m_span = NSA_CMP_LEN // NSA_CMP_STRIDE
    ratio = NSA_SEL_BLOCK // NSA_CMP_STRIDE
    n_sel = -(-L // NSA_SEL_BLOCK)
    pp = jnp.pad(imp, ((0, 0), (0, 0), (0, 0), (m_span - 1, ratio * n_sel - n_cmp)))
    p_slc = pp[..., 0:ratio * n_sel:ratio]
    for u in range(1, ratio + m_span - 1):
        p_slc = p_slc + pp[..., u:u + ratio * n_sel:ratio]
    j = jnp.arange(n_sel)[None, None, None, :]
    j_cur = (pos // NSA_SEL_BLOCK)[None, :, None, None]
    forced = (j == 0) | (j == j_cur) | (j == j_cur - 1)
    score = jnp.where(forced, FORCE_SCORE, jnp.where(j <= j_cur, p_slc, -jnp.inf))
    _, sel = lax.top_k(score, min(NSA_TOPN, n_sel))
    sel_valid = sel <= j_cur
    o_sel = _block_gather_attention(qg, sel[:, :, :, None], sel_valid[:, :, :, None], pos, NSA_SEL_BLOCK,
                                    ks, vs, past_sk, past_sv)
    o_win = _window_attention(q, kw, vw, win_k_buf, win_v_buf, NSA_WINDOW).reshape(B, T, G, R, hd)
    g = gates.reshape(B, T, G, R, 3)
    out = g[..., 0:1] * o_cmp + g[..., 1:2] * o_sel + g[..., 2:3] * o_win
    return out.reshape(B, T, H * hd)


def _mixer_ab(h, pos, w_in, conv_w, sinks, w_out, conv_buf, k_buf, v_buf):
    B, T, _ = h.shape
    hq, hk = SWA_HEADS * HEAD_DIM, SWA_KV_HEADS * HEAD_DIM
    gate_b, gate_c, u_in, q, k, v = _split(h @ w_in, (CONV_DIM, CONV_DIM, CONV_DIM, hq, hk, hk))
    conv_y, conv_new = _short_conv(gate_c * u_in, conv_w, conv_buf)
    y_conv = gate_b * conv_y
    q = _rope(q.reshape(B, T, SWA_HEADS, HEAD_DIM), pos)
    k = _rope(k.reshape(B, T, SWA_KV_HEADS, HEAD_DIM), pos)
    v = v.reshape(B, T, SWA_KV_HEADS, HEAD_DIM)
    y_attn = _window_attention(q, k, v, k_buf, v_buf, SWA_WINDOW, sinks)
    out = jnp.concatenate([y_conv, y_attn.reshape(B, T, hq)], axis=-1) @ w_out
    return out, (conv_new, _window_tail(k_buf, k, SWA_WINDOW), _window_tail(v_buf, v, SWA_WINDOW))


def _mixer_cd(h, pos, p0, w_in, ck_w1, ck_b1, ck_w2, cv_w1, cv_b1, cv_w2, w_out, get):
    B, T, _ = h.shape
    hd = HEAD_DIM
    sizes = (MOBA_HEADS * hd, MOBA_KV_HEADS * hd, MOBA_KV_HEADS * hd, NSA_HEADS * hd) + (NSA_KV_HEADS * hd,) * 6 + (NSA_HEADS * 3,)
    qm, km, vm, qn, kc, vc, ks, vs, kw, vw, gl = _split(h @ w_in, sizes)
    qm = _rope(qm.reshape(B, T, MOBA_HEADS, hd), pos)
    km = _rope(km.reshape(B, T, MOBA_KV_HEADS, hd), pos)
    vm = vm.reshape(B, T, MOBA_KV_HEADS, hd)
    qn = _rope(qn.reshape(B, T, NSA_HEADS, hd), pos)
    kc = kc.reshape(B, T, NSA_KV_HEADS, hd)
    vc = vc.reshape(B, T, NSA_KV_HEADS, hd)
    ks = _rope(ks.reshape(B, T, NSA_KV_HEADS, hd), pos)
    vs = vs.reshape(B, T, NSA_KV_HEADS, hd)
    kw = _rope(kw.reshape(B, T, NSA_KV_HEADS, hd), pos)
    vw = vw.reshape(B, T, NSA_KV_HEADS, hd)
    gates = jax.nn.sigmoid(gl.reshape(B, T, NSA_HEADS, 3))
    y_moba = _moba(qm, km, vm, get('moba_k'), get('moba_v'), pos).reshape(B, T, MOBA_HEADS * hd)
    y_nsa = _nsa(qn, kc, vc, ks, vs, kw, vw, gates, pos, p0, (ck_w1, ck_b1, ck_w2), (cv_w1, cv_b1, cv_w2),
                 get('cmp_k'), get('cmp_v'), get('sel_k'), get('sel_v'), get('win_k'), get('win_v'))
    out = jnp.concatenate([y_moba, y_nsa], axis=-1) @ w_out
    win_k_new = _window_tail(get('win_k'), kw, NSA_WINDOW)
    win_v_new = _window_tail(get('win_v'), vw, NSA_WINDOW)
    return out, (km, vm, kc, vc, ks, vs, win_k_new, win_v_new)


def _forward(x, p0, past, prm):
    T = x.shape[1]
    pos = p0 + jnp.arange(T, dtype=jnp.int32)
    get = (lambda name: None) if past is None else (lambda name: past[name])
    states = ()
    for l in range(DEPTH):
        x = x + 0.5 * _swiglu(_rmsnorm(x, prm['norm_ffn_a'][l]), prm['ffn_a_gate'][l], prm['ffn_a_up'][l], prm['ffn_a_down'][l])
        h = _rmsnorm(x, prm['norm_mix'][l])
        if l % 2 == 0:
            mix, st = _mixer_ab(h, pos, prm['l0_w_in'], prm['l0_conv_w'], prm['l0_sinks'], prm['l0_w_out'],
                                get('conv'), get('swa_k'), get('swa_v'))
        else:
            mix, st = _mixer_cd(h, pos, p0, prm['l1_w_in'], prm['l1_cmp_k_w1'], prm['l1_cmp_k_b1'], prm['l1_cmp_k_w2'],
                                prm['l1_cmp_v_w1'], prm['l1_cmp_v_b1'], prm['l1_cmp_v_w2'], prm['l1_w_out'], get)
        states = states + st
        x = x + mix
        x = x + 0.5 * _swiglu(_rmsnorm(x, prm['norm_ffn_b'][l]), prm['ffn_b_gate'][l], prm['ffn_b_up'][l], prm['ffn_b_down'][l])
    return _rmsnorm(x, prm['norm_final']), states


def setup_inputs(seed: int = 0) -> dict:
    key = jax.random.key(seed)
    ks = iter(jax.random.split(key, 64))

    def nrm(shape, scale=1.0):
        return jax.random.normal(next(ks), shape, jnp.float32) * scale

    n_pages = PAST_LEN // PAGE_SIZE
    n_used = DEC_BATCH * n_pages
    n_pool = n_used + max(1, n_used // 4)
    page_table = jax.random.permutation(next(ks), n_pool)[:n_used].reshape(DEC_BATCH, n_pages).astype(jnp.int32)
    swa_rows = min(SWA_WINDOW, PAST_LEN)
    win_rows = min(NSA_WINDOW, PAST_LEN)
    pool_shape = (n_pool, PAGE_SIZE, MOBA_KV_HEADS, HEAD_DIM)
    nsa_pool_shape = (n_pool, PAGE_SIZE, NSA_KV_HEADS, HEAD_DIM)
    return {
        'x_prompt': nrm((BATCH, SEQ, D_MODEL)),
        'x_sample': nrm((DEC_BATCH, DEC_SEQ, D_MODEL)),
        'state_l0_conv': nrm((DEC_BATCH, CONV_WIDTH - 1, CONV_DIM)),
        'cache_l0_swa_k': nrm((DEC_BATCH, swa_rows, SWA_KV_HEADS, HEAD_DIM)),
        'cache_l0_swa_v': nrm((DEC_BATCH, swa_rows, SWA_KV_HEADS, HEAD_DIM)),
        'cache_l1_moba_k': nrm(pool_shape),
        'cache_l1_moba_v': nrm(pool_shape),
        'cache_l1_nsa_cmp_k': nrm(nsa_pool_shape),
        'cache_l1_nsa_cmp_v': nrm(nsa_pool_shape),
        'cache_l1_nsa_sel_k': nrm(nsa_pool_shape),
        'cache_l1_nsa_sel_v': nrm(nsa_pool_shape),
        'cache_l1_nsa_win_k': nrm((DEC_BATCH, win_rows, NSA_KV_HEADS, HEAD_DIM)),
        'cache_l1_nsa_win_v': nrm((DEC_BATCH, win_rows, NSA_KV_HEADS, HEAD_DIM)),
        'page_table': page_table,
        'norm_ffn_a': 1.0 + nrm((DEPTH, D_MODEL), 0.05),
        'ffn_a_gate': nrm((DEPTH, D_MODEL, D_FF), D_MODEL ** -0.5),
        'ffn_a_up': nrm((DEPTH, D_MODEL, D_FF), D_MODEL ** -0.5),
        'ffn_a_down': nrm((DEPTH, D_FF, D_MODEL), D_FF ** -0.5),
        'norm_mix': 1.0 + nrm((DEPTH, D_MODEL), 0.05),
        'norm_ffn_b': 1.0 + nrm((DEPTH, D_MODEL), 0.05),
        'ffn_b_gate': nrm((DEPTH, D_MODEL, D_FF), D_MODEL ** -0.5),
        'ffn_b_up': nrm((DEPTH, D_MODEL, D_FF), D_MODEL ** -0.5),
        'ffn_b_down': nrm((DEPTH, D_FF, D_MODEL), D_FF ** -0.5),
        'norm_final': 1.0 + nrm((D_MODEL,), 0.05),
        'l0_w_in': nrm((D_MODEL, D_IN_AB), D_MODEL ** -0.5),
        'l0_conv_w': nrm((CONV_WIDTH, CONV_DIM), CONV_WIDTH ** -0.5),
        'l0_sinks': nrm((SWA_HEADS,), 0.5),
        'l0_w_out': nrm((D_MIX_AB, D_MODEL), D_MIX_AB ** -0.5),
        'l1_w_in': nrm((D_MODEL, D_IN_CD), D_MODEL ** -0.5),
        'l1_cmp_k_w1': nrm((NSA_CMP_LEN, HEAD_DIM, NSA_CMP_HIDDEN), (NSA_CMP_LEN * HEAD_DIM) ** -0.5),
        'l1_cmp_k_b1': nrm((NSA_CMP_HIDDEN,), 0.02),
        'l1_cmp_k_w2': nrm((NSA_CMP_HIDDEN, HEAD_DIM), NSA_CMP_HIDDEN ** -0.5),
        'l1_cmp_v_w1': nrm((NSA_CMP_LEN, HEAD_DIM, NSA_CMP_HIDDEN), (NSA_CMP_LEN * HEAD_DIM) ** -0.5),
        'l1_cmp_v_b1': nrm((NSA_CMP_HIDDEN,), 0.02),
        'l1_cmp_v_w2': nrm((NSA_CMP_HIDDEN, HEAD_DIM), NSA_CMP_HIDDEN ** -0.5),
        'l1_w_out': nrm((D_MIX_CD, D_MODEL), D_MIX_CD ** -0.5),
    }


def reference(x_prompt, x_sample, state_l0_conv, cache_l0_swa_k, cache_l0_swa_v, cache_l1_moba_k, cache_l1_moba_v,
              cache_l1_nsa_cmp_k, cache_l1_nsa_cmp_v, cache_l1_nsa_sel_k, cache_l1_nsa_sel_v, cache_l1_nsa_win_k,
              cache_l1_nsa_win_v, page_table, norm_ffn_a, ffn_a_gate, ffn_a_up, ffn_a_down, norm_mix, norm_ffn_b,
              ffn_b_gate, ffn_b_up, ffn_b_down, norm_final, l0_w_in, l0_conv_w, l0_sinks, l0_w_out, l1_w_in,
              l1_cmp_k_w1, l1_cmp_k_b1, l1_cmp_k_w2, l1_cmp_v_w1, l1_cmp_v_b1, l1_cmp_v_w2, l1_w_out):
    prm = {
        'norm_ffn_a': norm_ffn_a, 'ffn_a_gate': ffn_a_gate, 'ffn_a_up': ffn_a_up, 'ffn_a_down': ffn_a_down,
        'norm_mix': norm_mix, 'norm_ffn_b': norm_ffn_b, 'ffn_b_gate': ffn_b_gate, 'ffn_b_up': ffn_b_up,
        'ffn_b_down': ffn_b_down, 'norm_final': norm_final, 'l0_w_in': l0_w_in, 'l0_conv_w': l0_conv_w,
        'l0_sinks': l0_sinks, 'l0_w_out': l0_w_out, 'l1_w_in': l1_w_in, 'l1_cmp_k_w1': l1_cmp_k_w1,
        'l1_cmp_k_b1': l1_cmp_k_b1, 'l1_cmp_k_w2': l1_cmp_k_w2, 'l1_cmp_v_w1': l1_cmp_v_w1,
        'l1_cmp_v_b1': l1_cmp_v_b1, 'l1_cmp_v_w2': l1_cmp_v_w2, 'l1_w_out': l1_w_out,
    }
    past_len = page_table.shape[1] * cache_l1_moba_k.shape[1]
    sample_past = {
        'conv': state_l0_conv, 'swa_k': cache_l0_swa_k, 'swa_v': cache_l0_swa_v,
        'moba_k': (cache_l1_moba_k, page_table), 'moba_v': (cache_l1_moba_v, page_table),
        'cmp_k': (cache_l1_nsa_cmp_k, page_table), 'cmp_v': (cache_l1_nsa_cmp_v, page_table),
        'sel_k': (cache_l1_nsa_sel_k, page_table), 'sel_v': (cache_l1_nsa_sel_v, page_table),
        'win_k': cache_l1_nsa_win_k, 'win_v': cache_l1_nsa_win_v,
    }
    y_prompt, (conv_p, swa_k_p, swa_v_p, moba_k_p, moba_v_p, cmp_k_p, cmp_v_p, sel_k_p, sel_v_p, win_k_p, win_v_p) = _forward(x_prompt, 0, None, prm)
    y_sample, (conv_s, swa_k_s, swa_v_s, moba_k_s, moba_v_s, cmp_k_s, cmp_v_s, sel_k_s, sel_v_s, win_k_s, win_v_s) = _forward(x_sample, past_len, sample_past, prm)
    return (y_prompt, y_sample, conv_p, conv_s, swa_k_p, swa_v_p, swa_k_s, swa_v_s,
            moba_k_p, moba_v_p, moba_k_s, moba_v_s, cmp_k_p, cmp_v_p, cmp_k_s, cmp_v_s,
            sel_k_p, sel_v_p, sel_k_s, sel_v_s, win_k_p, win_v_p, win_k_s, win_v_s)
```

```python
import functools
import math

import jax
import jax.numpy as jnp
from jax import lax
from jax.experimental import pallas as pl
from jax.experimental.pallas import tpu as pltpu

f32 = jnp.float32
bf16 = jnp.bfloat16

HEAD_DIM = 64
N_HEADS = 8
N_KV = 2
ROPE_THETA = 10000.0
NORM_EPS = 1e-5
CONV_WIDTH = 3
SWA_WINDOW = 128
MOBA_BLOCK = 256
MOBA_TOPK = 3
NSA_CMP_LEN = 32
NSA_CMP_STRIDE = 16
NSA_SEL_BLOCK = 64
NSA_TOPN = 16
NSA_WINDOW = 512
NEG_INF = -1e30
FORCE_SCORE = 1e9

LANES = 128
SUBLANES = 8
VMEM_LIMIT_BYTES = 60 * 1024 * 1024

FFN_ROWS = 512
PAGES_PER_STEP = 16
ATTN_ROWS = 256

_ARB = "arbitrary"


def _params(n_axes):
    return pltpu.CompilerParams(dimension_semantics=(_ARB,) * n_axes, vmem_limit_bytes=VMEM_LIMIT_BYTES)


def _log2(n):
    assert n > 0 and n & (n - 1) == 0, n
    return n.bit_length() - 1


def _dot(a, b):
    return jnp.dot(a, b, preferred_element_type=f32)


def _dot_nt(a, b):
    return lax.dot_general(a, b, (((1,), (1,)), ((), ())), preferred_element_type=f32)


def _split3(x):
    hi = x.astype(bf16)
    r1 = x - hi.astype(f32)
    mid = r1.astype(bf16)
    lo = (r1 - mid.astype(f32)).astype(bf16)
    return hi, mid, lo


def _rms(x, g):
    return x * lax.rsqrt(jnp.mean(x * x, axis=-1, keepdims=True) + NORM_EPS) * g


def _gate_expand(gates, comp):
    tm = gates.shape[0]
    head = jnp.right_shift(lax.broadcasted_iota(jnp.int32, (tm, N_HEADS * HEAD_DIM), 1), _log2(HEAD_DIM))
    out = jnp.zeros((tm, N_HEADS * HEAD_DIM), f32)
    for h in range(N_HEADS):
        col = gates[:, 3 * h + comp:3 * h + comp + 1]
        out = jnp.where(head == h, col, out)
    return out


def _ffn_body(mode, final, nf, *refs):
    it = iter(refs)
    x_ref = next(it)
    if mode == 1:
        pa_ref, pb_ref, wo_ref = next(it), next(it), next(it)
    elif mode == 2:
        om_ref, oc_ref, os_ref, ow_ref, gt_ref, wo_ref = (next(it) for _ in range(6))
    g_ref, wg_ref, wu_ref, wd_ref = next(it), next(it), next(it), next(it)
    gf_ref = next(it) if final else None
    o_ref, h_scr, acc_scr = next(it), next(it), next(it)
    j = pl.program_id(1)

    @pl.when(j == 0)
    def _():
        x1 = x_ref[...]
        if mode == 1:
            half = pa_ref.shape[1]
            x1 = x1 + _dot(pa_ref[...].astype(bf16), wo_ref[0:half, :])
            x1 = x1 + _dot(pb_ref[...].astype(bf16), wo_ref[half:2 * half, :])
        elif mode == 2:
            half = om_ref.shape[1]
            gates = gt_ref[...]
            y_nsa = (_gate_expand(gates, 0) * oc_ref[...] + _gate_expand(gates, 1) * os_ref[...]
                     + _gate_expand(gates, 2) * ow_ref[...])
            x1 = x1 + _dot(om_ref[...].astype(bf16), wo_ref[0:half, :])
            x1 = x1 + _dot(y_nsa.astype(bf16), wo_ref[half:2 * half, :])
        acc_scr[...] = x1
        h_scr[...] = _rms(x1, g_ref[...]).astype(bf16)

    h = h_scr[...]
    gate = _dot(h, wg_ref[...])
    up = _dot(h, wu_ref[...])
    act = gate * (1.0 / (1.0 + jnp.exp(-gate))) * up
    acc_scr[...] += 0.5 * _dot(act.astype(bf16), wd_ref[...])

    @pl.when(j == nf - 1)
    def _():
        y = acc_scr[...]
        if final:
            y = _rms(y, gf_ref[...])
        o_ref[...] = y


def _ffn_chunk(d_ff):
    best = LANES
    for c in range(LANES, 1536 + 1, LANES):
        if d_ff % c == 0:
            best = c
    return best if d_ff % LANES == 0 else d_ff


def _ffn(x, g, wg, wu, wd, mix=None, final_g=None):
    n, d = x.shape
    d_ff = wg.shape[1]
    tm = min(FFN_ROWS, n)
    tf = _ffn_chunk(d_ff)
    nf = d_ff // tf
    row = lambda i, j: (i, 0)
    const = lambda i, j: (0, 0)
    args, specs = [x], [pl.BlockSpec((tm, d), row)]
    mode = 0
    if mix is not None:
        mode = 1 if len(mix) == 3 else 2
        for a in mix[:-1]:
            args.append(a)
            specs.append(pl.BlockSpec((tm, a.shape[1]), row))
        args.append(mix[-1])
        specs.append(pl.BlockSpec(mix[-1].shape, const))
    args += [g.reshape(1, d), wg, wu, wd]
    specs += [pl.BlockSpec((1, d), const), pl.BlockSpec((d, tf), lambda i, j: (0, j)),
              pl.BlockSpec((d, tf), lambda i, j: (0, j)), pl.BlockSpec((tf, d), lambda i, j: (j, 0))]
    if final_g is not None:
        args.append(final_g.reshape(1, d))
        specs.append(pl.BlockSpec((1, d), const))
    return pl.pallas_call(
        functools.partial(_ffn_body, mode, final_g is not None, nf),
        grid=(n // tm, nf),
        in_specs=specs,
        out_specs=pl.BlockSpec((tm, d), row),
        out_shape=jax.ShapeDtypeStruct((n, d), f32),
        scratch_shapes=[pltpu.VMEM((tm, d), bf16), pltpu.VMEM((tm, d), f32)],
        compiler_params=_params(2),
        name="ffn",
    )(*args)


def _rope_tables(pos):
    half = HEAD_DIM // 2
    inv_freq = ROPE_THETA ** (-jnp.arange(half, dtype=f32) * 2.0 / HEAD_DIM)
    ang = pos.astype(f32)[:, None] * inv_freq[None, :]
    cos, sin = jnp.cos(ang), jnp.sin(ang)
    zero = jnp.zeros_like(sin)
    reps = LANES // HEAD_DIM
    return (jnp.concatenate([cos, cos] * reps, axis=1), jnp.concatenate([-sin, zero] * reps, axis=1),
            jnp.concatenate([zero, sin] * reps, axis=1))


def _rope(x, cos, sa, sb):
    half = HEAD_DIM // 2
    outs = []
    for c in range(x.shape[1] // LANES):
        xc = x[:, c * LANES:(c + 1) * LANES]
        outs.append(xc * cos + pltpu.roll(xc, LANES - half, 1) * sa + pltpu.roll(xc, half, 1) * sb)
    return outs[0] if len(outs) == 1 else jnp.concatenate(outs, axis=1)


def _proj_ab_body(cdim, x_ref, g_ref, w_ref, cos_ref, sa_ref, sb_ref, gb_ref, cu_ref, q_ref, k_ref, v_ref):
    h = _rms(x_ref[...], g_ref[...]).astype(bf16)
    cos, sa, sb = cos_ref[...], sa_ref[...], sb_ref[...]
    seg = lambda off, width: _dot(h, w_ref[:, off:off + width])
    hq, hk = q_ref.shape[1], k_ref.shape[1]
    gb_ref[...] = seg(0, cdim)
    cu_ref[...] = seg(cdim, cdim) * seg(2 * cdim, cdim)
    q_ref[...] = _rope(seg(3 * cdim, hq), cos, sa, sb)
    k_ref[...] = _rope(seg(3 * cdim + hq, hk), cos, sa, sb)
    v_ref[...] = seg(3 * cdim + hq + hk, hk)


def _proj_cd_body(x_ref, g_ref, w_ref, cos_ref, sa_ref, sb_ref,
                  qm_ref, km_ref, vm_ref, qn_ref, kc_ref, vc_ref, ks_ref, vs_ref, kw_ref, vw_ref, gt_ref):
    h = _rms(x_ref[...], g_ref[...]).astype(bf16)
    cos, sa, sb = cos_ref[...], sa_ref[...], sb_ref[...]
    off = 0
    for ref, kind in ((qm_ref, "rope"), (km_ref, "rope"), (vm_ref, ""), (qn_ref, "rope"), (kc_ref, ""),
                      (vc_ref, ""), (ks_ref, "rope"), (vs_ref, ""), (kw_ref, "rope"), (vw_ref, ""),
                      (gt_ref, "sigmoid")):
        width = ref.shape[1]
        z = _dot(h, w_ref[:, off:off + width])
        if kind == "rope":
            z = _rope(z, cos, sa, sb)
        elif kind == "sigmoid":
            z = 1.0 / (1.0 + jnp.exp(-z))
        ref[...] = z
        off += width


def _proj(body, x, g, w, tables, n_tab_tiles, widths):
    n, d = x.shape
    tm = min(FFN_ROWS, n)
    row = lambda i: (i, 0)
    const = lambda i: (0, 0)
    tab = lambda i: (i % n_tab_tiles, 0)
    return pl.pallas_call(
        body,
        grid=(n // tm,),
        in_specs=[pl.BlockSpec((tm, d), row), pl.BlockSpec((1, d), const), pl.BlockSpec(w.shape, const)]
        + [pl.BlockSpec((tm, LANES), tab)] * 3,
        out_specs=[pl.BlockSpec((tm, wd), row) for wd in widths],
        out_shape=[jax.ShapeDtypeStruct((n, wd), f32) for wd in widths],
        compiler_params=_params(1),
        name="proj",
    )(x, g.reshape(1, d), w, *tables)


def _build_qbd(q, scale):
    tq = q.shape[0]
    lo = lax.broadcasted_iota(jnp.int32, (tq, LANES), 1) < HEAD_DIM
    per = N_HEADS // N_KV
    pieces = []
    for h in range(N_HEADS):
        sl = q[:, LANES * (h // 2):LANES * (h // 2) + LANES]
        src_hi, dst_hi = (h % 2 == 1), (h // per == 1)
        if src_hi != dst_hi:
            sl = pltpu.roll(sl, HEAD_DIM, 1)
        pieces.append(jnp.where(lo != dst_hi, sl, 0.0))
    return (jnp.concatenate(pieces, axis=0) * scale).astype(bf16)


def _unstack(o, tq):
    lo = lax.broadcasted_iota(jnp.int32, (tq, LANES), 1) < HEAD_DIM
    per = N_HEADS // N_KV
    outs = []
    for c in range(N_HEADS // 2):
        a = o[(2 * c) * tq:(2 * c + 1) * tq]
        b = o[(2 * c + 1) * tq:(2 * c + 2) * tq]
        if (2 * c) // per == 1:
            a = pltpu.roll(a, HEAD_DIM, 1)
        if (2 * c + 1) // per == 0:
            b = pltpu.roll(b, HEAD_DIM, 1)
        outs.append(jnp.where(lo, a, b))
    return jnp.concatenate(outs, axis=1)


def _row_token(rows, tq, width):
    _log2(tq)
    return lax.broadcasted_iota(jnp.int32, (rows, width), 0) & (tq - 1)


def _attn_update(qbd, k, v, allowed, m, l, acc):
    s = jnp.where(allowed, _dot_nt(qbd, k.astype(bf16)), NEG_INF)
    m_new = jnp.maximum(m, jnp.max(s, axis=-1, keepdims=True))
    alpha = jnp.exp(m - m_new)
    p = jnp.where(allowed, jnp.exp(s - m_new), 0.0)
    l_new = alpha * l + jnp.sum(p, axis=-1, keepdims=True)
    acc_new = alpha * acc + _dot(p.astype(bf16), v.astype(bf16))
    return m_new, l_new, acc_new


def _pad_rows(x, rows):
    if x.shape[0] >= rows:
        return x
    return jnp.concatenate([x, jnp.zeros((rows - x.shape[0], x.shape[1]), x.dtype)], axis=0)


def _expand_mask(m_rows, key_pos, blk_shift):
    nbp, tk = m_rows.shape[1], key_pos.shape[1]
    blk = jnp.right_shift(key_pos, blk_shift)
    e = jnp.where(lax.broadcasted_iota(jnp.int32, (nbp, tk), 0) == blk, 1.0, 0.0).astype(bf16)
    return _dot(m_rows, e) > 0.5


def _win_body(tq, window, hb, nh, hist_always, with_conv, with_sinks, *refs):
    it = iter(refs)
    q_ref, kc_ref, vc_ref = (next(it) for _ in range(3))
    kh_refs = [next(it) for _ in range(nh)]
    vh_refs = [next(it) for _ in range(nh)]
    if with_conv:
        gb_ref, cu_ref, cup_ref, cw_ref = (next(it) for _ in range(4))
    sink_ref = next(it) if with_sinks else None
    o_ref = next(it)
    oc_ref = next(it) if with_conv else None
    i = pl.program_id(1)
    rows = N_HEADS * tq

    qbd = _build_qbd(q_ref[...], HEAD_DIM ** -0.5)
    m = jnp.full((rows, 1), NEG_INF, f32)
    l = jnp.zeros((rows, 1), f32)
    acc = jnp.zeros((rows, LANES), f32)
    t_h = _row_token(rows, tq, hb)
    c_h = lax.broadcasted_iota(jnp.int32, (rows, hb), 1)
    for j in range(nh):
        ok = (c_h - (nh - j) * hb >= t_h - window)
        if not hist_always:
            ok = ok & (i * (tq // hb) >= nh - j)
        m, l, acc = _attn_update(qbd, kh_refs[j][...], vh_refs[j][...], ok, m, l, acc)
    hist_ok = True if hist_always else (i > 0)
    tk = max(tq, LANES)
    t_c = _row_token(rows, tq, tk)
    c_c = lax.broadcasted_iota(jnp.int32, (rows, tk), 1)
    allowed = (c_c <= t_c) & (c_c >= t_c - window) & (c_c < tq)
    m, l, acc = _attn_update(qbd, _pad_rows(kc_ref[...], tk), _pad_rows(vc_ref[...], tk), allowed, m, l, acc)
    if with_sinks:
        head = jnp.right_shift(lax.broadcasted_iota(jnp.int32, (rows, 1), 0), _log2(tq))
        sink = jnp.zeros((rows, 1), f32)
        for h in range(N_HEADS):
            sink = jnp.where(head == h, sink_ref[h], sink)
        m_f = jnp.maximum(m, sink)
        scale = jnp.exp(m - m_f)
        out = acc * scale / (l * scale + jnp.exp(sink - m_f))
    else:
        out = acc / l
    o_ref[...] = _unstack(out, tq)

    if with_conv:
        cu = cu_ref[...]
        prev = jnp.where(hist_ok, cup_ref[...], 0.0)
        r = lax.broadcasted_iota(jnp.int32, cu.shape, 0)
        p1 = prev[SUBLANES - 1:SUBLANES, :]
        p2 = prev[SUBLANES - 2:SUBLANES - 1, :]
        cu_m1 = jnp.where(r == 0, p1, pltpu.roll(cu, 1, 0))
        cu_m2 = jnp.where(r == 0, p2, jnp.where(r == 1, p1, pltpu.roll(cu, 2, 0)))
        cw = cw_ref[...]
        oc_ref[...] = gb_ref[...] * (cu_m2 * cw[0:1, :] + cu_m1 * cw[1:2, :] + cu * cw[2:3, :])


def _window_attn(q, k, v, k_hist, v_hist, nb, nt, tq, window, conv=None, sinks=None):
    n = q.shape[0]
    hq = q.shape[1]
    per_seq = nt * tq
    cur = lambda b, i: (b * nt + i, 0)
    if k_hist is None:
        hist_always = False
        hb = min(window, tq)
        nh = window // hb
        assert tq % hb == 0 and window % hb == 0

        def hist_map(j):
            return lambda b, i: (b * (per_seq // hb) + jnp.maximum(i * (tq // hb) - (nh - j), 0), 0)

        k_hist, v_hist = k, v
    else:
        hist_always = True
        hb, nh = window, 1
        hist_map = lambda j: (lambda b, i: (b, 0))
    args = [q, k, v] + [k_hist] * nh + [v_hist] * nh
    specs = ([pl.BlockSpec((tq, hq), cur), pl.BlockSpec((tq, LANES), cur), pl.BlockSpec((tq, LANES), cur)]
             + [pl.BlockSpec((hb, LANES), hist_map(j)) for j in range(nh)] * 2)
    out_shape = [jax.ShapeDtypeStruct((n, hq), f32)]
    out_specs = [pl.BlockSpec((tq, hq), cur)]
    if conv is not None:
        gb, cu, cu_prev, cw = conv
        cdim = gb.shape[1]
        if cu_prev is None:
            prev = lambda b, i: (jnp.maximum((b * per_seq + i * tq) // SUBLANES - 1, 0), 0)
            cu_prev = cu
        else:
            prev = lambda b, i: (b, 0)
        args += [gb, cu, cu_prev, cw]
        specs += [pl.BlockSpec((tq, cdim), cur), pl.BlockSpec((tq, cdim), cur),
                  pl.BlockSpec((SUBLANES, cdim), prev), pl.BlockSpec(cw.shape, lambda b, i: (0, 0))]
        out_shape.append(jax.ShapeDtypeStruct((n, cdim), f32))
        out_specs.append(pl.BlockSpec((tq, cdim), cur))
    if sinks is not None:
        args.append(sinks)
        specs.append(pl.BlockSpec(memory_space=pltpu.SMEM))
    return pl.pallas_call(
        functools.partial(_win_body, tq, window, hb, nh, hist_always, conv is not None, sinks is not None),
        grid=(nb, nt),
        in_specs=specs,
        out_specs=out_specs,
        out_shape=out_shape,
        compiler_params=_params(2),
        name="window_attn",
    )(*args)


def _cmp_weights(w1, b1, w2):
    n_half = NSA_CMP_LEN // NSA_CMP_STRIDE
    hid = w1.shape[2]
    w1h = w1.reshape(n_half, NSA_CMP_STRIDE, HEAD_DIM, hid)
    big = jnp.zeros((NSA_CMP_STRIDE, N_KV * HEAD_DIM, n_half * N_KV * hid), f32)
    for hh in range(n_half):
        for g in range(N_KV):
            big = big.at[:, g * HEAD_DIM:(g + 1) * HEAD_DIM,
                         hh * N_KV * hid + g * hid:hh * N_KV * hid + (g + 1) * hid].set(w1h[hh])
    w2bd = jnp.zeros((N_KV * hid, N_KV * HEAD_DIM), f32)
    for g in range(N_KV):
        w2bd = w2bd.at[g * hid:(g + 1) * hid, g * HEAD_DIM:(g + 1) * HEAD_DIM].set(w2)
    return big.astype(bf16), jnp.tile(b1, N_KV).reshape(1, N_KV * hid), w2bd.astype(bf16)


def _cmp_finish(hp, n_tok, b1, w2bd):
    rows = hp.shape[0]
    hp1 = pltpu.roll(hp[:, LANES:], rows - 1, 0)
    hid = jax.nn.gelu(hp[:, :LANES] + hp1 + b1)
    return _dot(hid.astype(bf16), w2bd)[:n_tok]


def _cmp_prompt_body(n_chunk, x_ref, w1_ref, b1_ref, w2_ref, o_ref):
    hp = jnp.zeros((n_chunk, 2 * LANES), f32)
    for s in range(NSA_CMP_STRIDE):
        hp = hp + _dot(x_ref[pl.ds(s, n_chunk, stride=NSA_CMP_STRIDE), :].astype(bf16), w1_ref[s])
    r = lax.broadcasted_iota(jnp.int32, (n_chunk, LANES), 0)
    hp1 = jnp.where(r == n_chunk - 1, 0.0, pltpu.roll(hp[:, LANES:], n_chunk - 1, 0))
    hid = jax.nn.gelu(hp[:, :LANES] + hp1 + b1_ref[...])
    o_ref[...] = _dot(hid.astype(bf16), w2_ref[...])


def _compress_prompt(x, nb, t, wts):
    w1, b1, w2 = wts
    n_chunk = t // NSA_CMP_STRIDE
    c3 = lambda b: (0, 0, 0)
    c2 = lambda b: (0, 0)
    return pl.pallas_call(
        functools.partial(_cmp_prompt_body, n_chunk),
        grid=(nb,),
        in_specs=[pl.BlockSpec((t, LANES), lambda b: (b, 0)), pl.BlockSpec(w1.shape, c3),
                  pl.BlockSpec(b1.shape, c2), pl.BlockSpec(w2.shape, c2)],
        out_specs=pl.BlockSpec((n_chunk, LANES), lambda b: (b, 0)),
        out_shape=jax.ShapeDtypeStruct((nb * n_chunk, LANES), f32),
        compiler_params=_params(1),
        name="compress_prompt",
    )(x, w1, b1, w2)


def _page_specs(n_pages, pps, page):
    specs = []
    for i in range(pps):
        specs.append(pl.BlockSpec((1, page, LANES),
                                  lambda b, c, tab, i=i: (tab[b * n_pages + c * pps + i], 0, 0)))
    return specs


def _cmp_sample_body(pps, nc, page, n_past_chunk, *refs):
    tab_ref = refs[0]
    pages = refs[1:1 + pps]
    xn_ref, w1_ref, b1_ref, w2_ref, o_ref, hp_scr = refs[1 + pps:]
    del tab_ref
    c = pl.program_id(1)
    cpp = page // NSA_CMP_STRIDE
    hp = jnp.zeros((pps * cpp, 2 * LANES), f32)
    for s in range(NSA_CMP_STRIDE):
        xs = jnp.concatenate([p[0, pl.ds(s, cpp, stride=NSA_CMP_STRIDE), :] for p in pages], axis=0)
        hp = hp + _dot(xs.astype(bf16), w1_ref[s])
    hp_scr[pl.ds(pl.multiple_of(c * (pps * cpp), SUBLANES), pps * cpp), :] = hp

    @pl.when(c == nc - 1)
    def _():
        xn = xn_ref[...].astype(bf16)
        r = lax.broadcasted_iota(jnp.int32, (NSA_CMP_STRIDE, 2 * LANES), 0)
        new = jnp.zeros((NSA_CMP_STRIDE, 2 * LANES), f32)
        for s in range(NSA_CMP_STRIDE):
            new = new + jnp.where(r == s, _dot(xn, w1_ref[s]), 0.0)
        new = jnp.sum(new, axis=0, keepdims=True)
        r8 = lax.broadcasted_iota(jnp.int32, (SUBLANES, 2 * LANES), 0)
        hp_scr[n_past_chunk:n_past_chunk + SUBLANES, :] = jnp.where(r8 == 0, new, 0.0)
        o_ref[...] = _cmp_finish(hp_scr[...], n_past_chunk, b1_ref[...], w2_ref[...])


def _compress_sample(pool, table, x_new, nb, t_new, wts):
    w1, b1, w2 = wts
    page = pool.shape[1]
    n_pages = table.shape[0] // nb
    pps = min(PAGES_PER_STEP, n_pages)
    nc = n_pages // pps
    n_past_chunk = n_pages * page // NSA_CMP_STRIDE
    assert t_new <= NSA_CMP_STRIDE
    xn = jnp.pad(x_new.reshape(nb, t_new, LANES), ((0, 0), (0, NSA_CMP_STRIDE - t_new), (0, 0)))
    xn = xn.reshape(nb * NSA_CMP_STRIDE, LANES)
    c3 = lambda b, c, tab: (0, 0, 0)
    c2 = lambda b, c, tab: (0, 0)
    grid_spec = pltpu.PrefetchScalarGridSpec(
        num_scalar_prefetch=1,
        grid=(nb, nc),
        in_specs=_page_specs(n_pages, pps, page)
        + [pl.BlockSpec((NSA_CMP_STRIDE, LANES), lambda b, c, tab: (b, 0)), pl.BlockSpec(w1.shape, c3),
           pl.BlockSpec(b1.shape, c2), pl.BlockSpec(w2.shape, c2)],
        out_specs=pl.BlockSpec((n_past_chunk, LANES), lambda b, c, tab: (b, 0)),
        scratch_shapes=[pltpu.VMEM((n_past_chunk + SUBLANES, 2 * LANES), f32)],
    )
    return pl.pallas_call(
        functools.partial(_cmp_sample_body, pps, nc, page, n_past_chunk),
        grid_spec=grid_spec,
        out_shape=jax.ShapeDtypeStruct((nb * n_past_chunk, LANES), f32),
        compiler_params=_params(2),
        name="compress_sample",
    )(table, *([pool] * pps), xn, w1, b1, w2)


def _means_prompt_body(n_blk, nbp, k_ref, o_ref):
    t = k_ref.shape[0]
    sums = jnp.sum(k_ref[...].reshape(n_blk, t // n_blk, LANES), axis=1)
    o_ref[...] = _pad_rows(sums * (1.0 / MOBA_BLOCK), nbp)


def _means_prompt(k, nb, t, nbp):
    n_blk = t // MOBA_BLOCK
    return pl.pallas_call(
        functools.partial(_means_prompt_body, n_blk, nbp),
        grid=(nb,),
        in_specs=[pl.BlockSpec((t, LANES), lambda b: (b, 0))],
        out_specs=pl.BlockSpec((nbp, LANES), lambda b: (b, 0)),
        out_shape=jax.ShapeDtypeStruct((nb * nbp, LANES), f32),
        compiler_params=_params(1),
        name="moba_means_prompt",
    )(k)


def _means_sample_body(pps, nc, n_past_blk, nbp, *refs):
    pages = refs[1:1 + pps]
    kn_ref, o_ref = refs[1 + pps:]
    c = pl.program_id(1)
    per = pps // 2

    @pl.when(c == 0)
    def _():
        o_ref[...] = jnp.zeros((nbp, LANES), f32)

    sums = [jnp.sum(p[0], axis=0, keepdims=True) for p in pages]
    blk = jnp.concatenate([sums[2 * i] + sums[2 * i + 1] for i in range(per)], axis=0)
    o_ref[pl.ds(pl.multiple_of(c * per, per), per), :] = blk * (1.0 / MOBA_BLOCK)

    @pl.when(c == nc - 1)
    def _():
        new = jnp.sum(kn_ref[...], axis=0, keepdims=True) * (1.0 / MOBA_BLOCK)
        r8 = lax.broadcasted_iota(jnp.int32, (SUBLANES, LANES), 0)
        o_ref[n_past_blk:n_past_blk + SUBLANES, :] = jnp.where(r8 == 0, new, 0.0)


def _means_sample(pool, table, k_new, nb, t_new, nbp):
    page = pool.shape[1]
    n_pages = table.shape[0] // nb
    pps = min(PAGES_PER_STEP, n_pages)
    nc = n_pages // pps
    assert MOBA_BLOCK == 2 * page and t_new <= MOBA_BLOCK and (pps // 2) % SUBLANES == 0
    n_past_blk = n_pages * page // MOBA_BLOCK
    grid_spec = pltpu.PrefetchScalarGridSpec(
        num_scalar_prefetch=1,
        grid=(nb, nc),
        in_specs=_page_specs(n_pages, pps, page) + [pl.BlockSpec((t_new, LANES), lambda b, c, tab: (b, 0))],
        out_specs=pl.BlockSpec((nbp, LANES), lambda b, c, tab: (b, 0)),
    )
    return pl.pallas_call(
        functools.partial(_means_sample_body, pps, nc, n_past_blk, nbp),
        grid_spec=grid_spec,
        out_shape=jax.ShapeDtypeStruct((nb * nbp, LANES), f32),
        compiler_params=_params(2),
        name="moba_means_sample",
    )(table, *([pool] * pps), k_new)


def _topk_cols(score, k):
    n = score.shape[0]
    ridx = lax.broadcasted_iota(jnp.int32, score.shape, 0)
    sel = jnp.zeros(score.shape, f32)
    work = score
    for _ in range(k):
        mx = jnp.max(work, axis=0, keepdims=True)
        cand = (work == mx) & (mx > -jnp.inf)
        first = jnp.min(jnp.where(cand, ridx, n), axis=0, keepdims=True)
        pick = ridx == first
        sel = jnp.where(pick, 1.0, sel)
        work = jnp.where(pick, -jnp.inf, work)
    return sel


def _moba_sel_body(tq, p0, nbp, q_ref, mean_ref, o_ref):
    i = pl.program_id(1)
    rows = N_HEADS * tq
    cols = max(rows, LANES)
    qbd = _pad_rows(_build_qbd(q_ref[...], 1.0).astype(f32), cols)
    q_hi = qbd.astype(bf16)
    q_lo = (qbd - q_hi.astype(f32)).astype(bf16)
    mean = mean_ref[...]
    m_hi = mean.astype(bf16)
    m_lo = (mean - m_hi.astype(f32)).astype(bf16)
    s = _dot_nt(m_hi, q_hi) + _dot_nt(m_hi, q_lo) + _dot_nt(m_lo, q_hi)
    n_idx = lax.broadcasted_iota(jnp.int32, (nbp, cols), 0)
    t = lax.broadcasted_iota(jnp.int32, (nbp, cols), 1) & (tq - 1)
    cur = jnp.right_shift(p0 + i * tq + t, _log2(MOBA_BLOCK))
    sel = _topk_cols(jnp.where(n_idx < cur, s, -jnp.inf), MOBA_TOPK)
    sel = jnp.where(n_idx == cur, 1.0, sel)
    sel_t = sel.T
    for h in range(N_HEADS):
        o_ref[h] = sel_t[h * tq:(h + 1) * tq]


def _moba_select(q, means, nb, nt, tq, p0, nbp):
    n, hq = q.shape
    return pl.pallas_call(
        functools.partial(_moba_sel_body, tq, p0, nbp),
        grid=(nb, nt),
        in_specs=[pl.BlockSpec((tq, hq), lambda b, i: (b * nt + i, 0)),
                  pl.BlockSpec((nbp, LANES), lambda b, i: (b, 0))],
        out_specs=pl.BlockSpec((N_HEADS, tq, nbp), lambda b, i: (0, b * nt + i, 0)),
        out_shape=jax.ShapeDtypeStruct((N_HEADS, n, nbp), f32),
        compiler_params=_params(2),
        name="moba_select",
    )(q, means)


def _cmp_sel_body(tq, p0, n_ck, n_cmp, nsp, q_ref, ck_ref, cv_ref, o_ref, m_ref):
    i = pl.program_id(1)
    rows = N_HEADS * tq
    per = N_HEADS // N_KV
    qbd = _build_qbd(q_ref[...], HEAD_DIM ** -0.5)
    n_idx = lax.broadcasted_iota(jnp.int32, (rows, n_ck), 1)
    pos = p0 + i * tq + _row_token(rows, tq, n_ck)
    valid = (n_idx * NSA_CMP_STRIDE + (NSA_CMP_LEN - 1) <= pos) & (n_idx < n_cmp)
    s = jnp.where(valid, _dot_nt(qbd, ck_ref[...].astype(bf16)), NEG_INF)
    mx = jnp.max(s, axis=-1, keepdims=True)
    e = jnp.where(valid, jnp.exp(s - mx), 0.0)
    den = jnp.sum(e, axis=-1, keepdims=True)
    p = e * jnp.where(den > 0.0, 1.0 / den, 0.0)
    o_ref[...] = _unstack(_dot(p.astype(bf16), cv_ref[...].astype(bf16)), tq)

    cols = max(tq, LANES)
    j_idx = lax.broadcasted_iota(jnp.int32, (nsp, n_ck), 0)
    n_of = lax.broadcasted_iota(jnp.int32, (nsp, n_ck), 1)
    ratio = NSA_SEL_BLOCK // NSA_CMP_STRIDE
    m_span = NSA_CMP_LEN // NSA_CMP_STRIDE
    lo = ratio * j_idx - (m_span - 1)
    a_t = jnp.where((n_of >= lo) & (n_of <= lo + ratio + m_span - 2), 1.0, 0.0).astype(bf16)
    j_col = lax.broadcasted_iota(jnp.int32, (nsp, cols), 0)
    t_col = lax.broadcasted_iota(jnp.int32, (nsp, cols), 1)
    j_cur = jnp.right_shift(p0 + i * tq + t_col, _log2(NSA_SEL_BLOCK))
    forced = (j_col == 0) | (j_col == j_cur) | (j_col == j_cur - 1)
    for g in range(N_KV):
        imp = p[(g * per) * tq:(g * per + 1) * tq]
        for r in range(1, per):
            imp = imp + p[(g * per + r) * tq:(g * per + r + 1) * tq]
        imp = _pad_rows(imp, cols)
        hi, mid, lo3 = _split3(imp)
        p_slc = _dot_nt(a_t, hi) + _dot_nt(a_t, mid) + _dot_nt(a_t, lo3)
        score = jnp.where(forced, FORCE_SCORE, jnp.where(j_col <= j_cur, p_slc, -jnp.inf))
        sel_t = _topk_cols(score, NSA_TOPN).T[:tq]
        for r in range(per):
            m_ref[g * per + r] = sel_t


def _cmp_select(q, ck, cv, nb, nt, tq, p0, n_ck, n_cmp, nsp):
    n, hq = q.shape
    cur = lambda b, i: (b * nt + i, 0)
    return pl.pallas_call(
        functools.partial(_cmp_sel_body, tq, p0, n_ck, n_cmp, nsp),
        grid=(nb, nt),
        in_specs=[pl.BlockSpec((tq, hq), cur), pl.BlockSpec((n_ck, LANES), lambda b, i: (b, 0)),
                  pl.BlockSpec((n_ck, LANES), lambda b, i: (b, 0))],
        out_specs=[pl.BlockSpec((tq, hq), cur),
                   pl.BlockSpec((N_HEADS, tq, nsp), lambda b, i: (0, b * nt + i, 0))],
        out_shape=[jax.ShapeDtypeStruct((n, hq), f32), jax.ShapeDtypeStruct((N_HEADS, n, nsp), f32)],
        compiler_params=_params(2),
        name="cmp_select",
    )(q, ck, cv)


def _blk_prompt_body(tq, blk_shift, q_ref, k_ref, v_ref, mask_ref, o_ref, m_scr, l_scr, acc_scr):
    i = pl.program_id(1)
    rows = N_HEADS * tq
    nbp = mask_ref.shape[2]
    qbd = _build_qbd(q_ref[...], HEAD_DIM ** -0.5)
    m_rows = mask_ref[...].reshape(rows, nbp).astype(bf16)
    m_scr[...] = jnp.full((rows, 1), NEG_INF, f32)
    l_scr[...] = jnp.zeros((rows, 1), f32)
    acc_scr[...] = jnp.zeros((rows, LANES), f32)
    q_pos = i * tq + _row_token(rows, tq, tq)

    def step(j, carry):
        start = pl.multiple_of(j * tq, tq)
        k_pos = j * tq + lax.broadcasted_iota(jnp.int32, (1, tq), 1)
        allowed = _expand_mask(m_rows, k_pos, blk_shift) & (k_pos <= q_pos)
        m, l, acc = _attn_update(qbd, k_ref[pl.ds(start, tq), :], v_ref[pl.ds(start, tq), :], allowed,
                                 m_scr[...], l_scr[...], acc_scr[...])
        m_scr[...] = m
        l_scr[...] = l
        acc_scr[...] = acc
        return carry

    lax.fori_loop(0, i + 1, step, 0)
    o_ref[...] = _unstack(acc_scr[...] / l_scr[...], tq)


def _block_attn_prompt(q, k, v, mask, nb, nt, tq, blk):
    n, hq = q.shape
    t = nt * tq
    nbp = mask.shape[2]
    rows = N_HEADS * tq
    cur = lambda b, i: (b * nt + i, 0)
    seq = lambda b, i: (b, 0)
    return pl.pallas_call(
        functools.partial(_blk_prompt_body, tq, int(math.log2(blk))),
        grid=(nb, nt),
        in_specs=[pl.BlockSpec((tq, hq), cur), pl.BlockSpec((t, LANES), seq), pl.BlockSpec((t, LANES), seq),
                  pl.BlockSpec((N_HEADS, tq, nbp), lambda b, i: (0, b * nt + i, 0))],
        out_specs=pl.BlockSpec((tq, hq), cur),
        out_shape=jax.ShapeDtypeStruct((n, hq), f32),
        scratch_shapes=[pltpu.VMEM((rows, 1), f32), pltpu.VMEM((rows, 1), f32), pltpu.VMEM((rows, LANES), f32)],
        compiler_params=_params(2),
        name="block_attn_prompt",
    )(q, k, v, mask)


def _blk_sample_body(pps, nc, page, tq, n_past, blk_shift, *refs):
    kp = refs[1:1 + pps]
    vp = refs[1 + pps:1 + 2 * pps]
    q_ref, kn_ref, vn_ref, mask_ref, o_ref, qbd_scr, m_scr, l_scr, acc_scr = refs[1 + 2 * pps:]
    c = pl.program_id(1)
    rows = N_HEADS * tq
    nbp = mask_ref.shape[2]

    @pl.when(c == 0)
    def _():
        qbd_scr[...] = _build_qbd(q_ref[...], HEAD_DIM ** -0.5)
        m_scr[...] = jnp.full((rows, 1), NEG_INF, f32)
        l_scr[...] = jnp.zeros((rows, 1), f32)
        acc_scr[...] = jnp.zeros((rows, LANES), f32)

    qbd = qbd_scr[...]
    m_rows = mask_ref[...].reshape(rows, nbp).astype(bf16)
    tk = pps * page
    k_pos = c * tk + lax.broadcasted_iota(jnp.int32, (1, tk), 1)
    allowed = _expand_mask(m_rows, k_pos, blk_shift)
    k = jnp.concatenate([p[0] for p in kp], axis=0)
    v = jnp.concatenate([p[0] for p in vp], axis=0)
    m, l, acc = _attn_update(qbd, k, v, allowed, m_scr[...], l_scr[...], acc_scr[...])

    @pl.when(c < nc - 1)
    def _():
        m_scr[...] = m
        l_scr[...] = l
        acc_scr[...] = acc

    @pl.when(c == nc - 1)
    def _():
        tn = max(tq, LANES)
        idx = lax.broadcasted_iota(jnp.int32, (1, tn), 1)
        allowed_n = (_expand_mask(m_rows, n_past + idx, blk_shift) & (idx <= _row_token(rows, tq, tn))
                     & (idx < tq))
        m2, l2, acc2 = _attn_update(qbd, _pad_rows(kn_ref[...], tn), _pad_rows(vn_ref[...], tn), allowed_n,
                                    m, l, acc)
        o_ref[...] = _unstack(acc2 / l2, tq)


def _block_attn_sample(q, pool_k, pool_v, table, k_new, v_new, mask, nb, tq, blk):
    n, hq = q.shape
    page = pool_k.shape[1]
    n_pages = table.shape[0] // nb
    pps = min(PAGES_PER_STEP, n_pages)
    nc = n_pages // pps
    nbp = mask.shape[2]
    rows = N_HEADS * tq
    cur = lambda b, c, tab: (b, 0)
    grid_spec = pltpu.PrefetchScalarGridSpec(
        num_scalar_prefetch=1,
        grid=(nb, nc),
        in_specs=_page_specs(n_pages, pps, page) * 2
        + [pl.BlockSpec((tq, hq), cur), pl.BlockSpec((tq, LANES), cur), pl.BlockSpec((tq, LANES), cur),
           pl.BlockSpec((N_HEADS, tq, nbp), lambda b, c, tab: (0, b, 0))],
        out_specs=pl.BlockSpec((tq, hq), cur),
        scratch_shapes=[pltpu.VMEM((rows, LANES), bf16), pltpu.VMEM((rows, 1), f32), pltpu.VMEM((rows, 1), f32),
                        pltpu.VMEM((rows, LANES), f32)],
    )
    return pl.pallas_call(
        functools.partial(_blk_sample_body, pps, nc, page, tq, n_pages * page, int(math.log2(blk))),
        grid_spec=grid_spec,
        out_shape=jax.ShapeDtypeStruct((n, hq), f32),
        compiler_params=_params(2),
        name="block_attn_sample",
    )(table, *([pool_k] * pps), *([pool_v] * pps), q, k_new, v_new, mask)


def _round_up(x, m):
    return -(-x // m) * m


def _forward(x, nb, t, p0, past, prm):
    n = nb * t
    prompt = past is None
    tm = min(FFN_ROWS, n)
    if prompt:
        pos = jnp.arange(t, dtype=jnp.int32)
        n_tab = t // tm
    else:
        pos = p0 + jnp.arange(tm, dtype=jnp.int32) % t
        n_tab = 1
    tables = _rope_tables(pos)

    x = _ffn(x, prm["norm_ffn_a"][0], *prm["ffn_a"][0])
    cdim = prm["l0_conv_w"].shape[1]
    hq, hk = N_HEADS * HEAD_DIM, N_KV * HEAD_DIM
    gb, cu, q, k, v = _proj(functools.partial(_proj_ab_body, cdim), x, prm["norm_mix"][0], prm["l0_w_in"],
                            tables, n_tab, (cdim, cdim, hq, hk, hk))
    if prompt:
        tq = min(ATTN_ROWS, t)
        y_attn, y_conv = _window_attn(q, k, v, None, None, nb, t // tq, tq, SWA_WINDOW,
                                      conv=(gb, cu, None, prm["l0_conv_w"]), sinks=prm["l0_sinks"])
    else:
        conv_buf = jnp.pad(past["conv"], ((0, 0), (SUBLANES - (CONV_WIDTH - 1), 0), (0, 0)))
        y_attn, y_conv = _window_attn(q, k, v, past["swa_k"].reshape(-1, hk), past["swa_v"].reshape(-1, hk),
                                      nb, 1, t, SWA_WINDOW,
                                      conv=(gb, cu, conv_buf.reshape(nb * SUBLANES, cdim), prm["l0_conv_w"]),
                                      sinks=prm["l0_sinks"])
    x = _ffn(x, prm["norm_ffn_b"][0], *prm["ffn_b"][0], mix=(y_conv, y_attn, prm["l0_w_out"]))
    states0 = (cu, k, v)

    x = _ffn(x, prm["norm_ffn_a"][1], *prm["ffn_a"][1])
    qm, km, vm, qn, kc, vc, ks, vs, kw, vw, gates = _proj(
        _proj_cd_body, x, prm["norm_mix"][1], prm["l1_w_in"], tables, n_tab,
        (hq, hk, hk, hq, hk, hk, hk, hk, hk, hk, LANES))
    if prompt:
        tq = min(ATTN_ROWS, t)
        nt = t // tq
        n_blk = t // MOBA_BLOCK
        nbp = _round_up(n_blk + 1, LANES)
        means = _means_prompt(km, nb, t, nbp)
        moba_mask = _moba_select(qm, means, nb, nt, tq, 0, nbp)
        o_moba = _block_attn_prompt(qm, km, vm, moba_mask, nb, nt, tq, MOBA_BLOCK)
        ck = _compress_prompt(kc, nb, t, prm["cmp_k"])
        cv = _compress_prompt(vc, nb, t, prm["cmp_v"])
        n_ck = t // NSA_CMP_STRIDE
        nsp = _round_up(-(-t // NSA_SEL_BLOCK), LANES)
        o_cmp, sel_mask = _cmp_select(qn, ck, cv, nb, nt, tq, 0, n_ck, n_ck - 1, nsp)
        o_sel = _block_attn_prompt(qn, ks, vs, sel_mask, nb, nt, tq, NSA_SEL_BLOCK)
        (o_win,) = _window_attn(qn, kw, vw, None, None, nb, nt, tq, NSA_WINDOW)
    else:
        table = past["table"]
        n_past = p0
        n_blk = n_past // MOBA_BLOCK
        nbp = _round_up(n_blk + 1, LANES)
        means = _means_sample(past["moba_k"], table, km, nb, t, nbp)
        moba_mask = _moba_select(qm, means, nb, 1, t, p0, nbp)
        o_moba = _block_attn_sample(qm, past["moba_k"], past["moba_v"], table, km, vm, moba_mask, nb, t,
                                    MOBA_BLOCK)
        ck = _compress_sample(past["cmp_k"], table, kc, nb, t, prm["cmp_k"])
        cv = _compress_sample(past["cmp_v"], table, vc, nb, t, prm["cmp_v"])
        n_ck = n_past // NSA_CMP_STRIDE
        nsp = _round_up(-(-(n_past + t) // NSA_SEL_BLOCK), LANES)
        o_cmp, sel_mask = _cmp_select(qn, ck, cv, nb, 1, t, p0, n_ck, n_ck, nsp)
        o_sel = _block_attn_sample(qn, past["sel_k"], past["sel_v"], table, ks, vs, sel_mask, nb, t,
                                   NSA_SEL_BLOCK)
        (o_win,) = _window_attn(qn, kw, vw, past["win_k"].reshape(-1, hk), past["win_v"].reshape(-1, hk),
                                nb, 1, t, NSA_WINDOW)
    x = _ffn(x, prm["norm_ffn_b"][1], *prm["ffn_b"][1], mix=(o_moba, o_cmp, o_sel, o_win, gates, prm["l1_w_out"]),
             final_g=prm["norm_final"])
    return x, states0 + (km, vm, kc, vc, ks, vs, kw, vw)


def _tail(buf, new, rows):
    full = new if buf is None else jnp.concatenate([buf.astype(new.dtype), new], axis=1)
    return full[:, full.shape[1] - rows:]


def kernel(x_prompt, x_sample, state_l0_conv, cache_l0_swa_k, cache_l0_swa_v, cache_l1_moba_k, cache_l1_moba_v, cache_l1_nsa_cmp_k, cache_l1_nsa_cmp_v, cache_l1_nsa_sel_k, cache_l1_nsa_sel_v, cache_l1_nsa_win_k, cache_l1_nsa_win_v, page_table, norm_ffn_a, ffn_a_gate, ffn_a_up, ffn_a_down, norm_mix, norm_ffn_b, ffn_b_gate, ffn_b_up, ffn_b_down, norm_final, l0_w_in, l0_conv_w, l0_sinks, l0_w_out, l1_w_in, l1_cmp_k_w1, l1_cmp_k_b1, l1_cmp_k_w2, l1_cmp_v_w1, l1_cmp_v_b1, l1_cmp_v_w2, l1_w_out):
    bp, tp, d = x_prompt.shape
    bs, ts, _ = x_sample.shape
    depth = norm_ffn_a.shape[0]
    assert depth == 2 and N_KV * HEAD_DIM == LANES
    n_pool, page = cache_l1_moba_k.shape[:2]
    p0 = page_table.shape[1] * page
    hk = N_KV * HEAD_DIM

    w1_pad = _round_up(l1_w_in.shape[1], LANES) - l1_w_in.shape[1]
    prm = {
        "norm_ffn_a": norm_ffn_a, "norm_mix": norm_mix, "norm_ffn_b": norm_ffn_b, "norm_final": norm_final,
        "ffn_a": [(ffn_a_gate[l].astype(bf16), ffn_a_up[l].astype(bf16), ffn_a_down[l].astype(bf16))
                  for l in range(depth)],
        "ffn_b": [(ffn_b_gate[l].astype(bf16), ffn_b_up[l].astype(bf16), ffn_b_down[l].astype(bf16))
                  for l in range(depth)],
        "l0_w_in": l0_w_in.astype(bf16), "l0_conv_w": l0_conv_w, "l0_sinks": l0_sinks,
        "l0_w_out": l0_w_out.astype(bf16),
        "l1_w_in": jnp.pad(l1_w_in, ((0, 0), (0, w1_pad))).astype(bf16), "l1_w_out": l1_w_out.astype(bf16),
        "cmp_k": _cmp_weights(l1_cmp_k_w1, l1_cmp_k_b1, l1_cmp_k_w2),
        "cmp_v": _cmp_weights(l1_cmp_v_w1, l1_cmp_v_b1, l1_cmp_v_w2),
    }
    pool = lambda a: a.reshape(n_pool, page, hk)
    past = {
        "conv": state_l0_conv, "swa_k": cache_l0_swa_k, "swa_v": cache_l0_swa_v,
        "moba_k": pool(cache_l1_moba_k), "moba_v": pool(cache_l1_moba_v),
        "cmp_k": pool(cache_l1_nsa_cmp_k), "cmp_v": pool(cache_l1_nsa_cmp_v),
        "sel_k": pool(cache_l1_nsa_sel_k), "sel_v": pool(cache_l1_nsa_sel_v),
        "win_k": cache_l1_nsa_win_k, "win_v": cache_l1_nsa_win_v,
        "table": page_table.reshape(-1),
    }
    y_p, st_p = _forward(x_prompt.reshape(bp * tp, d), bp, tp, 0, None, prm)
    y_s, st_s = _forward(x_sample.reshape(bs * ts, d), bs, ts, p0, past, prm)

    def states(st, nb, t, bufs):
        cu, k0, v0, km, vm, kc, vc, ks, vs, kw, vw = st
        kv = lambda a: a.reshape(nb, t, N_KV, HEAD_DIM)
        conv = _tail(bufs["conv"], cu.reshape(nb, t, -1), CONV_WIDTH - 1)
        swa_rows = min(SWA_WINDOW, t) if bufs["swa_k"] is None else bufs["swa_k"].shape[1]
        win_rows = min(NSA_WINDOW, t) if bufs["win_k"] is None else bufs["win_k"].shape[1]
        return (conv, _tail(bufs["swa_k"], kv(k0), swa_rows), _tail(bufs["swa_v"], kv(v0), swa_rows),
                kv(km), kv(vm), kv(kc), kv(vc), kv(ks), kv(vs),
                _tail(bufs["win_k"], kv(kw), win_rows), _tail(bufs["win_v"], kv(vw), win_rows))

    none = {"conv": None, "swa_k": None, "swa_v": None, "win_k": None, "win_v": None}
    sp = states(st_p, bp, tp, none)
    ss = states(st_s, bs, ts, past)
    conv_p, swa_k_p, swa_v_p, mk_p, mv_p, ck_p, cv_p, sk_p, sv_p, wk_p, wv_p = sp
    conv_s, swa_k_s, swa_v_s, mk_s, mv_s, ck_s, cv_s, sk_s, sv_s, wk_s, wv_s = ss
    return (y_p.reshape(bp, tp, d), y_s.reshape(bs, ts, d), conv_p, conv_s, swa_k_p, swa_v_p, swa_k_s, swa_v_s,
            mk_p, mv_p, mk_s, mv_s, ck_p, cv_p, ck_s, cv_s,
            sk_p, sv_p, sk_s, sv_s, wk_p, wv_p, wk_s, wv_s)
```

```python
import functools
import math

import jax
import jax.numpy as jnp
from jax import lax
from jax.experimental import pallas as pl
from jax.experimental.pallas import tpu as pltpu

f32 = jnp.float32
bf16 = jnp.bfloat16

HEAD_DIM = 64
N_HEADS = 8
N_KV = 2
ROPE_THETA = 10000.0
NORM_EPS = 1e-5
CONV_WIDTH = 3
SWA_WINDOW = 128
MOBA_BLOCK = 256
MOBA_TOPK = 3
NSA_CMP_LEN = 32
NSA_CMP_STRIDE = 16
NSA_SEL_BLOCK = 64
NSA_TOPN = 16
NSA_WINDOW = 512
NEG_INF = -1e30
FORCE_SCORE = 1e9

LANES = 128
SUBLANES = 8
VMEM_LIMIT_BYTES = 60 * 1024 * 1024

FFN_ROWS = 512
PAGES_PER_STEP = 16
ATTN_ROWS = 256
SOFTMAX_ROWS = 64

_ARB = "arbitrary"


def _params(n_axes):
    return pltpu.CompilerParams(dimension_semantics=(_ARB,) * n_axes, vmem_limit_bytes=VMEM_LIMIT_BYTES)


def _log2(n):
    assert n > 0 and n & (n - 1) == 0, n
    return n.bit_length() - 1


def _dot(a, b):
    return jnp.dot(a, b, preferred_element_type=f32)


def _dot_nt(a, b):
    return lax.dot_general(a, b, (((1,), (1,)), ((), ())), preferred_element_type=f32)


def _split3(x):
    hi = x.astype(bf16)
    r1 = x - hi.astype(f32)
    mid = r1.astype(bf16)
    lo = (r1 - mid.astype(f32)).astype(bf16)
    return hi, mid, lo


def _rms(x, g):
    return x * lax.rsqrt(jnp.mean(x * x, axis=-1, keepdims=True) + NORM_EPS) * g


def _gate_expand(gates, comp):
    tm = gates.shape[0]
    head = jnp.right_shift(lax.broadcasted_iota(jnp.int32, (tm, N_HEADS * HEAD_DIM), 1), _log2(HEAD_DIM))
    out = jnp.zeros((tm, N_HEADS * HEAD_DIM), f32)
    for h in range(N_HEADS):
        col = gates[:, 3 * h + comp:3 * h + comp + 1]
        out = jnp.where(head == h, col, out)
    return out


def _ffn_body(mode, final, nf, *refs):
    it = iter(refs)
    x_ref = next(it)
    if mode == 1:
        pa_ref, pb_ref, wo_ref = next(it), next(it), next(it)
    elif mode == 2:
        om_ref, oc_ref, os_ref, ow_ref, gt_ref, wo_ref = (next(it) for _ in range(6))
    g_ref, wg_ref, wu_ref, wd_ref = next(it), next(it), next(it), next(it)
    gf_ref = next(it) if final else None
    o_ref, h_scr, acc_scr = next(it), next(it), next(it)
    j = pl.program_id(1)

    @pl.when(j == 0)
    def _():
        x1 = x_ref[...]
        if mode == 1:
            half = pa_ref.shape[1]
            x1 = x1 + _dot(pa_ref[...].astype(bf16), wo_ref[0:half, :])
            x1 = x1 + _dot(pb_ref[...].astype(bf16), wo_ref[half:2 * half, :])
        elif mode == 2:
            half = om_ref.shape[1]
            gates = gt_ref[...]
            y_nsa = (_gate_expand(gates, 0) * oc_ref[...] + _gate_expand(gates, 1) * os_ref[...]
                     + _gate_expand(gates, 2) * ow_ref[...])
            x1 = x1 + _dot(om_ref[...].astype(bf16), wo_ref[0:half, :])
            x1 = x1 + _dot(y_nsa.astype(bf16), wo_ref[half:2 * half, :])
        acc_scr[...] = x1
        h_scr[...] = _rms(x1, g_ref[...]).astype(bf16)

    h = h_scr[...]
    gate = _dot(h, wg_ref[...])
    up = _dot(h, wu_ref[...])
    act = gate * (1.0 / (1.0 + jnp.exp(-gate))) * up
    acc_scr[...] += 0.5 * _dot(act.astype(bf16), wd_ref[...])

    @pl.when(j == nf - 1)
    def _():
        y = acc_scr[...]
        if final:
            y = _rms(y, gf_ref[...])
        o_ref[...] = y


def _ffn_chunk(d_ff):
    best = LANES
    for c in range(LANES, 1536 + 1, LANES):
        if d_ff % c == 0:
            best = c
    return best if d_ff % LANES == 0 else d_ff


def _ffn(x, g, wg, wu, wd, mix=None, final_g=None):
    n, d = x.shape
    d_ff = wg.shape[1]
    tm = min(FFN_ROWS, n)
    tf = _ffn_chunk(d_ff)
    nf = d_ff // tf
    row = lambda i, j: (i, 0)
    const = lambda i, j: (0, 0)
    args, specs = [x], [pl.BlockSpec((tm, d), row)]
    mode = 0
    if mix is not None:
        mode = 1 if len(mix) == 3 else 2
        for a in mix[:-1]:
            args.append(a)
            specs.append(pl.BlockSpec((tm, a.shape[1]), row))
        args.append(mix[-1])
        specs.append(pl.BlockSpec(mix[-1].shape, const))
    args += [g.reshape(1, d), wg, wu, wd]
    specs += [pl.BlockSpec((1, d), const), pl.BlockSpec((d, tf), lambda i, j: (0, j)),
              pl.BlockSpec((d, tf), lambda i, j: (0, j)), pl.BlockSpec((tf, d), lambda i, j: (j, 0))]
    if final_g is not None:
        args.append(final_g.reshape(1, d))
        specs.append(pl.BlockSpec((1, d), const))
    return pl.pallas_call(
        functools.partial(_ffn_body, mode, final_g is not None, nf),
        grid=(n // tm, nf),
        in_specs=specs,
        out_specs=pl.BlockSpec((tm, d), row),
        out_shape=jax.ShapeDtypeStruct((n, d), f32),
        scratch_shapes=[pltpu.VMEM((tm, d), bf16), pltpu.VMEM((tm, d), f32)],
        compiler_params=_params(2),
        name="ffn",
    )(*args)


def _rope_tables(pos):
    half = HEAD_DIM // 2
    inv_freq = ROPE_THETA ** (-jnp.arange(half, dtype=f32) * 2.0 / HEAD_DIM)
    ang = pos.astype(f32)[:, None] * inv_freq[None, :]
    cos, sin = jnp.cos(ang), jnp.sin(ang)
    zero = jnp.zeros_like(sin)
    reps = LANES // HEAD_DIM
    return (jnp.concatenate([cos, cos] * reps, axis=1), jnp.concatenate([-sin, zero] * reps, axis=1),
            jnp.concatenate([zero, sin] * reps, axis=1))


def _rope(x, cos, sa, sb):
    half = HEAD_DIM // 2
    outs = []
    for c in range(x.shape[1] // LANES):
        xc = x[:, c * LANES:(c + 1) * LANES]
        outs.append(xc * cos + pltpu.roll(xc, LANES - half, 1) * sa + pltpu.roll(xc, half, 1) * sb)
    return outs[0] if len(outs) == 1 else jnp.concatenate(outs, axis=1)


def _proj_ab_body(cdim, x_ref, g_ref, w_ref, cos_ref, sa_ref, sb_ref, gb_ref, cu_ref, q_ref, k_ref, v_ref):
    h = _rms(x_ref[...], g_ref[...]).astype(bf16)
    cos, sa, sb = cos_ref[...], sa_ref[...], sb_ref[...]
    seg = lambda off, width: _dot(h, w_ref[:, off:off + width])
    hq, hk = q_ref.shape[1], k_ref.shape[1]
    gb_ref[...] = seg(0, cdim)
    cu_ref[...] = seg(cdim, cdim) * seg(2 * cdim, cdim)
    q_ref[...] = _rope(seg(3 * cdim, hq), cos, sa, sb)
    k_ref[...] = _rope(seg(3 * cdim + hq, hk), cos, sa, sb)
    v_ref[...] = seg(3 * cdim + hq + hk, hk)


def _proj_cd_body(x_ref, g_ref, w_ref, cos_ref, sa_ref, sb_ref,
                  qm_ref, km_ref, vm_ref, qn_ref, kc_ref, vc_ref, ks_ref, vs_ref, kw_ref, vw_ref, gt_ref):
    h = _rms(x_ref[...], g_ref[...]).astype(bf16)
    cos, sa, sb = cos_ref[...], sa_ref[...], sb_ref[...]
    off = 0
    for ref, kind in ((qm_ref, "rope"), (km_ref, "rope"), (vm_ref, ""), (qn_ref, "rope"), (kc_ref, ""),
                      (vc_ref, ""), (ks_ref, "rope"), (vs_ref, ""), (kw_ref, "rope"), (vw_ref, ""),
                      (gt_ref, "sigmoid")):
        width = ref.shape[1]
        z = _dot(h, w_ref[:, off:off + width])
        if kind == "rope":
            z = _rope(z, cos, sa, sb)
        elif kind == "sigmoid":
            z = 1.0 / (1.0 + jnp.exp(-z))
        ref[...] = z
        off += width


def _proj(body, x, g, w, tables, n_tab_tiles, widths):
    n, d = x.shape
    tm = min(FFN_ROWS, n)
    row = lambda i: (i, 0)
    const = lambda i: (0, 0)
    tab = lambda i: (i % n_tab_tiles, 0)
    return pl.pallas_call(
        body,
        grid=(n // tm,),
        in_specs=[pl.BlockSpec((tm, d), row), pl.BlockSpec((1, d), const), pl.BlockSpec(w.shape, const)]
        + [pl.BlockSpec((tm, LANES), tab)] * 3,
        out_specs=[pl.BlockSpec((tm, wd), row) for wd in widths],
        out_shape=[jax.ShapeDtypeStruct((n, wd), f32) for wd in widths],
        compiler_params=_params(1),
        name="proj",
    )(x, g.reshape(1, d), w, *tables)


def _build_qbd(q, scale):
    tq = q.shape[0]
    lo = lax.broadcasted_iota(jnp.int32, (tq, LANES), 1) < HEAD_DIM
    per = N_HEADS // N_KV
    pieces = []
    for h in range(N_HEADS):
        sl = q[:, LANES * (h // 2):LANES * (h // 2) + LANES]
        src_hi, dst_hi = (h % 2 == 1), (h // per == 1)
        if src_hi != dst_hi:
            sl = pltpu.roll(sl, HEAD_DIM, 1)
        pieces.append(jnp.where(lo != dst_hi, sl, 0.0))
    return (jnp.concatenate(pieces, axis=0) * scale).astype(bf16)


def _unstack(o, tq):
    lo = lax.broadcasted_iota(jnp.int32, (tq, LANES), 1) < HEAD_DIM
    per = N_HEADS // N_KV
    outs = []
    for c in range(N_HEADS // 2):
        a = o[(2 * c) * tq:(2 * c + 1) * tq]
        b = o[(2 * c + 1) * tq:(2 * c + 2) * tq]
        if (2 * c) // per == 1:
            a = pltpu.roll(a, HEAD_DIM, 1)
        if (2 * c + 1) // per == 0:
            b = pltpu.roll(b, HEAD_DIM, 1)
        outs.append(jnp.where(lo, a, b))
    return jnp.concatenate(outs, axis=1)


def _row_token(rows, tq, width):
    _log2(tq)
    return lax.broadcasted_iota(jnp.int32, (rows, width), 0) & (tq - 1)


def _attn_update(qbd, k, v, allowed, m, l, acc, transposed=False):
    k, v = k.astype(bf16), v.astype(bf16)
    s = jnp.where(allowed, _dot(qbd, k) if transposed else _dot_nt(qbd, k), NEG_INF)
    m_new = jnp.maximum(m, jnp.max(s, axis=-1, keepdims=True))
    alpha = jnp.exp(m - m_new)
    p = jnp.where(allowed, jnp.exp(s - m_new), 0.0)
    l_new = alpha * l + jnp.sum(p, axis=-1, keepdims=True)
    p = p.astype(bf16)
    acc_new = alpha * acc + (_dot_nt(p, v) if transposed else _dot(p, v))
    return m_new, l_new, acc_new


def _pad_rows(x, rows):
    if x.shape[0] >= rows:
        return x
    return jnp.concatenate([x, jnp.zeros((rows - x.shape[0], x.shape[1]), x.dtype)], axis=0)


def _expand_mask(m_rows, key_pos, blk_shift):
    nbp, tk = m_rows.shape[1], key_pos.shape[1]
    blk = jnp.right_shift(key_pos, blk_shift)
    e = jnp.where(lax.broadcasted_iota(jnp.int32, (nbp, tk), 0) == blk, 1.0, 0.0).astype(bf16)
    return _dot(m_rows, e) > 0.5


def _win_body(tq, window, hb, nh, hist_always, with_conv, with_sinks, *refs):
    it = iter(refs)
    q_ref, kc_ref, vc_ref = (next(it) for _ in range(3))
    kh_refs = [next(it) for _ in range(nh)]
    vh_refs = [next(it) for _ in range(nh)]
    if with_conv:
        gb_ref, cu_ref, cup_ref, cw_ref = (next(it) for _ in range(4))
    sink_ref = next(it) if with_sinks else None
    o_ref = next(it)
    oc_ref = next(it) if with_conv else None
    i = pl.program_id(1)
    rows = N_HEADS * tq

    qbd = _build_qbd(q_ref[...], HEAD_DIM ** -0.5)
    m = jnp.full((rows, 1), NEG_INF, f32)
    l = jnp.zeros((rows, 1), f32)
    acc = jnp.zeros((rows, LANES), f32)
    t_h = _row_token(rows, tq, hb)
    c_h = lax.broadcasted_iota(jnp.int32, (rows, hb), 1)
    for j in range(nh):
        ok = (c_h - (nh - j) * hb >= t_h - window)
        if not hist_always:
            ok = ok & (i * (tq // hb) >= nh - j)
        m, l, acc = _attn_update(qbd, kh_refs[j][...], vh_refs[j][...], ok, m, l, acc, transposed=hist_always)
    hist_ok = True if hist_always else (i > 0)
    tk = max(tq, LANES)
    t_c = _row_token(rows, tq, tk)
    c_c = lax.broadcasted_iota(jnp.int32, (rows, tk), 1)
    allowed = (c_c <= t_c) & (c_c >= t_c - window) & (c_c < tq)
    m, l, acc = _attn_update(qbd, _pad_rows(kc_ref[...], tk), _pad_rows(vc_ref[...], tk), allowed, m, l, acc)
    if with_sinks:
        head = jnp.right_shift(lax.broadcasted_iota(jnp.int32, (rows, 1), 0), _log2(tq))
        sink = jnp.zeros((rows, 1), f32)
        for h in range(N_HEADS):
            sink = jnp.where(head == h, sink_ref[h], sink)
        m_f = jnp.maximum(m, sink)
        scale = jnp.exp(m - m_f)
        out = acc * scale / (l * scale + jnp.exp(sink - m_f))
    else:
        out = acc / l
    o_ref[...] = _unstack(out, tq)

    if with_conv:
        cu = cu_ref[...]
        prev = jnp.where(hist_ok, cup_ref[...], 0.0)
        r = lax.broadcasted_iota(jnp.int32, cu.shape, 0)
        p1 = prev[SUBLANES - 1:SUBLANES, :]
        p2 = prev[SUBLANES - 2:SUBLANES - 1, :]
        cu_m1 = jnp.where(r == 0, p1, pltpu.roll(cu, 1, 0))
        cu_m2 = jnp.where(r == 0, p2, jnp.where(r == 1, p1, pltpu.roll(cu, 2, 0)))
        cw = cw_ref[...]
        oc_ref[...] = gb_ref[...] * (cu_m2 * cw[0:1, :] + cu_m1 * cw[1:2, :] + cu * cw[2:3, :])


def _window_attn(q, k, v, k_hist, v_hist, nb, nt, tq, window, conv=None, sinks=None):
    n = q.shape[0]
    hq = q.shape[1]
    per_seq = nt * tq
    cur = lambda b, i: (b * nt + i, 0)
    if k_hist is None:
        hist_always = False
        hb = min(window, tq)
        nh = window // hb
        assert tq % hb == 0 and window % hb == 0

        def hist_map(j):
            return lambda b, i: (b * (per_seq // hb) + jnp.maximum(i * (tq // hb) - (nh - j), 0), 0)

        k_hist, v_hist = k, v
    else:
        hist_always = True
        hb, nh = window, 1
        hist_map = lambda j: (lambda b, i: (b, 0))
    hist_block = (LANES, hb) if hist_always else (hb, LANES)
    args = [q, k, v] + [k_hist] * nh + [v_hist] * nh
    specs = ([pl.BlockSpec((tq, hq), cur), pl.BlockSpec((tq, LANES), cur), pl.BlockSpec((tq, LANES), cur)]
             + [pl.BlockSpec(hist_block, hist_map(j)) for j in range(nh)] * 2)
    out_shape = [jax.ShapeDtypeStruct((n, hq), f32)]
    out_specs = [pl.BlockSpec((tq, hq), cur)]
    if conv is not None:
        gb, cu, cu_prev, cw = conv
        cdim = gb.shape[1]
        if cu_prev is None:
            prev = lambda b, i: (jnp.maximum((b * per_seq + i * tq) // SUBLANES - 1, 0), 0)
            cu_prev = cu
        else:
            prev = lambda b, i: (b, 0)
        args += [gb, cu, cu_prev, cw]
        specs += [pl.BlockSpec((tq, cdim), cur), pl.BlockSpec((tq, cdim), cur),
                  pl.BlockSpec((SUBLANES, cdim), prev), pl.BlockSpec(cw.shape, lambda b, i: (0, 0))]
        out_shape.append(jax.ShapeDtypeStruct((n, cdim), f32))
        out_specs.append(pl.BlockSpec((tq, cdim), cur))
    if sinks is not None:
        args.append(sinks)
        specs.append(pl.BlockSpec(memory_space=pltpu.SMEM))
    return pl.pallas_call(
        functools.partial(_win_body, tq, window, hb, nh, hist_always, conv is not None, sinks is not None),
        grid=(nb, nt),
        in_specs=specs,
        out_specs=out_specs,
        out_shape=out_shape,
        compiler_params=_params(2),
        name="window_attn",
    )(*args)


def _cmp_weights(w1, b1, w2):
    n_half = NSA_CMP_LEN // NSA_CMP_STRIDE
    hid = w1.shape[2]
    w1h = w1.reshape(n_half, NSA_CMP_STRIDE, HEAD_DIM, hid)
    big = jnp.zeros((NSA_CMP_STRIDE, N_KV * HEAD_DIM, n_half * N_KV * hid), f32)
    for hh in range(n_half):
        for g in range(N_KV):
            big = big.at[:, g * HEAD_DIM:(g + 1) * HEAD_DIM,
                         hh * N_KV * hid + g * hid:hh * N_KV * hid + (g + 1) * hid].set(w1h[hh])
    w2bd = jnp.zeros((N_KV * hid, N_KV * HEAD_DIM), f32)
    for g in range(N_KV):
        w2bd = w2bd.at[g * hid:(g + 1) * hid, g * HEAD_DIM:(g + 1) * HEAD_DIM].set(w2)
    return big.astype(bf16), jnp.tile(b1, N_KV).reshape(1, N_KV * hid), w2bd.astype(bf16)


def _cmp_finish(hp, n_tok, b1, w2bd):
    rows = hp.shape[0]
    hp1 = pltpu.roll(hp[:, LANES:], rows - 1, 0)
    hid = jax.nn.gelu(hp[:, :LANES] + hp1 + b1)
    return _dot(hid.astype(bf16), w2bd)[:n_tok]


def _cmp_prompt_body(n_chunk, x_ref, w1_ref, b1_ref, w2_ref, o_ref):
    hp = jnp.zeros((n_chunk, 2 * LANES), f32)
    for s in range(NSA_CMP_STRIDE):
        hp = hp + _dot(x_ref[pl.ds(s, n_chunk, stride=NSA_CMP_STRIDE), :].astype(bf16), w1_ref[s])
    r = lax.broadcasted_iota(jnp.int32, (n_chunk, LANES), 0)
    hp1 = jnp.where(r == n_chunk - 1, 0.0, pltpu.roll(hp[:, LANES:], n_chunk - 1, 0))
    hid = jax.nn.gelu(hp[:, :LANES] + hp1 + b1_ref[...])
    o_ref[...] = _dot(hid.astype(bf16), w2_ref[...])


def _compress_prompt(x, nb, t, wts):
    w1, b1, w2 = wts
    n_chunk = t // NSA_CMP_STRIDE
    c3 = lambda b: (0, 0, 0)
    c2 = lambda b: (0, 0)
    return pl.pallas_call(
        functools.partial(_cmp_prompt_body, n_chunk),
        grid=(nb,),
        in_specs=[pl.BlockSpec((t, LANES), lambda b: (b, 0)), pl.BlockSpec(w1.shape, c3),
                  pl.BlockSpec(b1.shape, c2), pl.BlockSpec(w2.shape, c2)],
        out_specs=pl.BlockSpec((n_chunk, LANES), lambda b: (b, 0)),
        out_shape=jax.ShapeDtypeStruct((nb * n_chunk, LANES), f32),
        compiler_params=_params(1),
        name="compress_prompt",
    )(x, w1, b1, w2)


def _page_specs(n_pages, pps, page):
    specs = []
    for i in range(pps):
        specs.append(pl.BlockSpec((1, LANES, page),
                                  lambda b, c, tab, i=i: (tab[b * n_pages + c * pps + i], 0, 0)))
    return specs


def _cmp_sample_body(pps, nc, page, n_past_chunk, *refs):
    tab_ref = refs[0]
    pages = refs[1:1 + pps]
    xn_ref, w1_ref, b1_ref, w2_ref, o_ref, hp_scr = refs[1 + pps:]
    del tab_ref
    c = pl.program_id(1)
    cpp = page // NSA_CMP_STRIDE
    out_r = lax.broadcasted_iota(jnp.int32, (page, page), 0)
    src_r = lax.broadcasted_iota(jnp.int32, (page, page), 1)
    src_of = (out_r & (cpp - 1)) * NSA_CMP_STRIDE + jnp.right_shift(out_r, _log2(cpp))
    perm = jnp.where(src_r == src_of, 1.0, 0.0).astype(bf16)
    regrouped = [_dot_nt(perm, p[0].astype(bf16)) for p in pages]
    hp = jnp.zeros((pps * cpp, 2 * LANES), f32)
    for s in range(NSA_CMP_STRIDE):
        xs = jnp.concatenate([x[s * cpp:(s + 1) * cpp] for x in regrouped], axis=0)
        hp = hp + _dot(xs.astype(bf16), w1_ref[s])
    hp_scr[pl.ds(pl.multiple_of(c * (pps * cpp), SUBLANES), pps * cpp), :] = hp

    @pl.when(c == nc - 1)
    def _():
        xn = xn_ref[...].astype(bf16)
        r = lax.broadcasted_iota(jnp.int32, (NSA_CMP_STRIDE, 2 * LANES), 0)
        new = jnp.zeros((NSA_CMP_STRIDE, 2 * LANES), f32)
        for s in range(NSA_CMP_STRIDE):
            new = new + jnp.where(r == s, _dot(xn, w1_ref[s]), 0.0)
        new = jnp.sum(new, axis=0, keepdims=True)
        r8 = lax.broadcasted_iota(jnp.int32, (SUBLANES, 2 * LANES), 0)
        hp_scr[n_past_chunk:n_past_chunk + SUBLANES, :] = jnp.where(r8 == 0, new, 0.0)
        o_ref[...] = _cmp_finish(hp_scr[...], n_past_chunk, b1_ref[...], w2_ref[...])


def _compress_sample(pool, table, x_new, nb, t_new, wts):
    w1, b1, w2 = wts
    page = pool.shape[2]
    n_pages = table.shape[0] // nb
    pps = min(PAGES_PER_STEP, n_pages)
    nc = n_pages // pps
    n_past_chunk = n_pages * page // NSA_CMP_STRIDE
    assert t_new <= NSA_CMP_STRIDE
    xn = jnp.pad(x_new.reshape(nb, t_new, LANES), ((0, 0), (0, NSA_CMP_STRIDE - t_new), (0, 0)))
    xn = xn.reshape(nb * NSA_CMP_STRIDE, LANES)
    c3 = lambda b, c, tab: (0, 0, 0)
    c2 = lambda b, c, tab: (0, 0)
    grid_spec = pltpu.PrefetchScalarGridSpec(
        num_scalar_prefetch=1,
        grid=(nb, nc),
        in_specs=_page_specs(n_pages, pps, page)
        + [pl.BlockSpec((NSA_CMP_STRIDE, LANES), lambda b, c, tab: (b, 0)), pl.BlockSpec(w1.shape, c3),
           pl.BlockSpec(b1.shape, c2), pl.BlockSpec(w2.shape, c2)],
        out_specs=pl.BlockSpec((n_past_chunk, LANES), lambda b, c, tab: (b, 0)),
        scratch_shapes=[pltpu.VMEM((n_past_chunk + SUBLANES, 2 * LANES), f32)],
    )
    return pl.pallas_call(
        functools.partial(_cmp_sample_body, pps, nc, page, n_past_chunk),
        grid_spec=grid_spec,
        out_shape=jax.ShapeDtypeStruct((nb * n_past_chunk, LANES), f32),
        compiler_params=_params(2),
        name="compress_sample",
    )(table, *([pool] * pps), xn, w1, b1, w2)


def _means_prompt_body(n_blk, nbp, k_ref, o_ref):
    t = k_ref.shape[0]
    sums = jnp.sum(k_ref[...].reshape(n_blk, t // n_blk, LANES), axis=1)
    o_ref[...] = _pad_rows(sums * (1.0 / MOBA_BLOCK), nbp).T


def _means_prompt(k, nb, t, nbp):
    n_blk = t // MOBA_BLOCK
    return pl.pallas_call(
        functools.partial(_means_prompt_body, n_blk, nbp),
        grid=(nb,),
        in_specs=[pl.BlockSpec((t, LANES), lambda b: (b, 0))],
        out_specs=pl.BlockSpec((LANES, nbp), lambda b: (b, 0)),
        out_shape=jax.ShapeDtypeStruct((nb * LANES, nbp), f32),
        compiler_params=_params(1),
        name="moba_means_prompt",
    )(k)


def _means_sample_body(pps, nc, n_past_blk, nbp, *refs):
    pages = refs[1:1 + pps]
    kn_ref, o_ref = refs[1 + pps:]
    c = pl.program_id(1)
    per = pps // 2

    lane = lax.broadcasted_iota(jnp.int32, (LANES, nbp), 1)
    @pl.when(c == 0)
    def _():
        o_ref[...] = jnp.zeros((LANES, nbp), f32)

    acc = o_ref[...]
    for i in range(per):
        col = jnp.sum(pages[2 * i][0] + pages[2 * i + 1][0], axis=1, keepdims=True) * (1.0 / MOBA_BLOCK)
        acc = jnp.where(lane == c * per + i, col, acc)
    new = jnp.sum(kn_ref[...], axis=1, keepdims=True) * (1.0 / MOBA_BLOCK)
    acc = jnp.where((lane == n_past_blk) & (c == nc - 1), new, acc)
    o_ref[...] = acc


def _means_sample(pool, table, k_new_t, nb, t_new, nbp):
    page = pool.shape[2]
    n_pages = table.shape[0] // nb
    pps = min(PAGES_PER_STEP, n_pages)
    nc = n_pages // pps
    assert MOBA_BLOCK == 2 * page and t_new <= MOBA_BLOCK and pps % 2 == 0
    n_past_blk = n_pages * page // MOBA_BLOCK
    grid_spec = pltpu.PrefetchScalarGridSpec(
        num_scalar_prefetch=1,
        grid=(nb, nc),
        in_specs=_page_specs(n_pages, pps, page) + [pl.BlockSpec((LANES, t_new), lambda b, c, tab: (b, 0))],
        out_specs=pl.BlockSpec((LANES, nbp), lambda b, c, tab: (b, 0)),
    )
    return pl.pallas_call(
        functools.partial(_means_sample_body, pps, nc, n_past_blk, nbp),
        grid_spec=grid_spec,
        out_shape=jax.ShapeDtypeStruct((nb * LANES, nbp), f32),
        compiler_params=_params(2),
        name="moba_means_sample",
    )(table, *([pool] * pps), k_new_t)


def _topk_cols(score, k):
    n = score.shape[0]
    ridx = lax.broadcasted_iota(jnp.int32, score.shape, 0)
    sel = jnp.zeros(score.shape, f32)
    work = score
    for _ in range(k):
        mx = jnp.max(work, axis=0, keepdims=True)
        cand = (work == mx) & (mx > -jnp.inf)
        first = jnp.min(jnp.where(cand, ridx, n), axis=0, keepdims=True)
        pick = ridx == first
        sel = jnp.where(pick, 1.0, sel)
        work = jnp.where(pick, -jnp.inf, work)
    return sel


def _moba_sel_body(tq, p0, nbp, q_ref, mean_ref, o_ref):
    i = pl.program_id(1)
    rows = N_HEADS * tq
    cols = max(rows, LANES)
    qbd = _pad_rows(_build_qbd(q_ref[...], 1.0).astype(f32), cols)
    q_hi = qbd.astype(bf16)
    q_lo = (qbd - q_hi.astype(f32)).astype(bf16)
    mean = mean_ref[...]
    m_hi = mean.astype(bf16)
    m_lo = (mean - m_hi.astype(f32)).astype(bf16)
    s = (_dot(q_hi, m_hi) + _dot(q_lo, m_hi) + _dot(q_hi, m_lo)).T
    n_idx = lax.broadcasted_iota(jnp.int32, (nbp, cols), 0)
    t = lax.broadcasted_iota(jnp.int32, (nbp, cols), 1) & (tq - 1)
    cur = jnp.right_shift(p0 + i * tq + t, _log2(MOBA_BLOCK))
    sel = _topk_cols(jnp.where(n_idx < cur, s, -jnp.inf), MOBA_TOPK)
    sel = jnp.where(n_idx == cur, 1.0, sel)
    sel_t = sel.T
    for h in range(N_HEADS):
        o_ref[h] = sel_t[h * tq:(h + 1) * tq]


def _moba_select(q, means, nb, nt, tq, p0, nbp):
    n, hq = q.shape
    return pl.pallas_call(
        functools.partial(_moba_sel_body, tq, p0, nbp),
        grid=(nb, nt),
        in_specs=[pl.BlockSpec((tq, hq), lambda b, i: (b * nt + i, 0)),
                  pl.BlockSpec((LANES, nbp), lambda b, i: (b, 0))],
        out_specs=pl.BlockSpec((N_HEADS, tq, nbp), lambda b, i: (0, b * nt + i, 0)),
        out_shape=jax.ShapeDtypeStruct((N_HEADS, n, nbp), f32),
        compiler_params=_params(2),
        name="moba_select",
    )(q, means)


def _cmp_sel_body(tq, p0, n_ck, n_cmp, nsp, q_ref, ck_ref, cv_ref, o_ref, m_ref):
    i = pl.program_id(1)
    rows = N_HEADS * tq
    per = N_HEADS // N_KV
    qbd = _build_qbd(q_ref[...], HEAD_DIM ** -0.5)
    n_idx = lax.broadcasted_iota(jnp.int32, (rows, n_ck), 1)
    pos = p0 + i * tq + _row_token(rows, tq, n_ck)
    valid = (n_idx * NSA_CMP_STRIDE + (NSA_CMP_LEN - 1) <= pos) & (n_idx < n_cmp)
    s = jnp.where(valid, _dot_nt(qbd, ck_ref[...].astype(bf16)), NEG_INF)
    mx = jnp.max(s, axis=-1, keepdims=True)
    e = jnp.where(valid, jnp.exp(s - mx), 0.0)
    den = jnp.sum(e, axis=-1, keepdims=True)
    p = e * jnp.where(den > 0.0, 1.0 / den, 0.0)
    o_ref[...] = _unstack(_dot(p.astype(bf16), cv_ref[...].astype(bf16)), tq)

    cols = max(tq, LANES)
    j_idx = lax.broadcasted_iota(jnp.int32, (nsp, n_ck), 0)
    n_of = lax.broadcasted_iota(jnp.int32, (nsp, n_ck), 1)
    ratio = NSA_SEL_BLOCK // NSA_CMP_STRIDE
    m_span = NSA_CMP_LEN // NSA_CMP_STRIDE
    lo = ratio * j_idx - (m_span - 1)
    a_t = jnp.where((n_of >= lo) & (n_of <= lo + ratio + m_span - 2), 1.0, 0.0).astype(bf16)
    j_col = lax.broadcasted_iota(jnp.int32, (nsp, cols), 0)
    t_col = lax.broadcasted_iota(jnp.int32, (nsp, cols), 1)
    j_cur = jnp.right_shift(p0 + i * tq + t_col, _log2(NSA_SEL_BLOCK))
    forced = (j_col == 0) | (j_col == j_cur) | (j_col == j_cur - 1)
    for g in range(N_KV):
        imp = p[(g * per) * tq:(g * per + 1) * tq]
        for r in range(1, per):
            imp = imp + p[(g * per + r) * tq:(g * per + r + 1) * tq]
        imp = _pad_rows(imp, cols)
        hi, mid, lo3 = _split3(imp)
        p_slc = _dot_nt(a_t, hi) + _dot_nt(a_t, mid) + _dot_nt(a_t, lo3)
        score = jnp.where(forced, FORCE_SCORE, jnp.where(j_col <= j_cur, p_slc, -jnp.inf))
        sel_t = _topk_cols(score, NSA_TOPN).T[:tq]
        for r in range(per):
            m_ref[g * per + r] = sel_t


def _cmp_select(q, ck, cv, nb, nt, tq, p0, n_ck, n_cmp, nsp):
    n, hq = q.shape
    cur = lambda b, i: (b * nt + i, 0)
    return pl.pallas_call(
        functools.partial(_cmp_sel_body, tq, p0, n_ck, n_cmp, nsp),
        grid=(nb, nt),
        in_specs=[pl.BlockSpec((tq, hq), cur), pl.BlockSpec((n_ck, LANES), lambda b, i: (b, 0)),
                  pl.BlockSpec((n_ck, LANES), lambda b, i: (b, 0))],
        out_specs=[pl.BlockSpec((tq, hq), cur),
                   pl.BlockSpec((N_HEADS, tq, nsp), lambda b, i: (0, b * nt + i, 0))],
        out_shape=[jax.ShapeDtypeStruct((n, hq), f32), jax.ShapeDtypeStruct((N_HEADS, n, nsp), f32)],
        compiler_params=_params(2),
        name="cmp_select",
    )(q, ck, cv)


def _blk_prompt_body(tq, blk_shift, q_ref, k_ref, v_ref, mask_ref, o_ref,
                     kaug_scr, vaug_scr, qaug_scr, s_scr, p_scr, a_scr, m_scr, acc_scr):
    i = pl.program_id(1)
    rows = N_HEADS * tq
    nbp = mask_ref.shape[2]
    t_len = k_ref.shape[0]

    @pl.when(i == 0)
    def _():
        key = lax.broadcasted_iota(jnp.int32, (nbp, t_len), 1)
        blk = lax.broadcasted_iota(jnp.int32, (nbp, t_len), 0)
        kaug_scr[:LANES, :] = k_ref[...].T.astype(bf16)
        kaug_scr[LANES:, :] = jnp.where(blk == jnp.right_shift(key, blk_shift), NEG_INF, 0.0).astype(bf16)
        ones = jnp.where(lax.broadcasted_iota(jnp.int32, (t_len, LANES), 1) == 0, 1.0, 0.0)
        vaug_scr[:, :LANES] = v_ref[...].astype(bf16)
        vaug_scr[:, LANES:] = ones.astype(bf16)

    unselected = 1.0 - mask_ref[...].reshape(rows, nbp)
    qbd = _build_qbd(q_ref[...], HEAD_DIM ** -0.5 * math.log2(math.e))
    qaug_scr[...] = jnp.concatenate([qbd, unselected.astype(bf16)], axis=1)
    m_scr[...] = jnp.full((rows, LANES), NEG_INF, f32)
    acc_scr[...] = jnp.zeros((rows, 2 * LANES), f32)

    def tile(j, causal):
        st = pl.multiple_of(j * tq, tq)
        s_scr[...] = _dot(qaug_scr[...], kaug_scr[:, pl.ds(st, tq)])
        for r in range(rows // SOFTMAX_ROWS):
            sl = slice(r * SOFTMAX_ROWS, (r + 1) * SOFTMAX_ROWS)
            s = s_scr[sl, :]
            if causal:
                t_row = (r * SOFTMAX_ROWS + lax.broadcasted_iota(jnp.int32, (SOFTMAX_ROWS, tq), 0)) & (tq - 1)
                s = jnp.where(lax.broadcasted_iota(jnp.int32, (SOFTMAX_ROWS, tq), 1) <= t_row, s, NEG_INF)
            m_old = m_scr[sl, :]
            m_new = jnp.maximum(m_old, jnp.max(s, axis=-1, keepdims=True))
            p_scr[sl, :] = jnp.exp2(s - jnp.concatenate([m_new] * (tq // LANES), axis=1)).astype(bf16)
            a_scr[sl, :] = jnp.exp2(m_old - m_new)
            m_scr[sl, :] = m_new
        a = a_scr[...]
        acc_scr[...] = jnp.concatenate([a, a], axis=1) * acc_scr[...] + _dot(p_scr[...], vaug_scr[pl.ds(st, tq), :])

    tile(i, True)
    lax.fori_loop(0, i, lambda j, carry: (tile(j, False), carry)[1], 0)
    acc = acc_scr[...]
    o_ref[...] = _unstack(acc[:, :LANES] / acc[:, LANES:LANES + 1], tq)


def _block_attn_prompt(q, k, v, mask, nb, nt, tq, blk):
    n, hq = q.shape
    t = nt * tq
    nbp = mask.shape[2]
    assert nbp == LANES
    rows = N_HEADS * tq
    cur = lambda b, i: (b * nt + i, 0)
    seq = lambda b, i: (b, 0)
    return pl.pallas_call(
        functools.partial(_blk_prompt_body, tq, _log2(blk)),
        grid=(nb, nt),
        in_specs=[pl.BlockSpec((tq, hq), cur), pl.BlockSpec((t, LANES), seq), pl.BlockSpec((t, LANES), seq),
                  pl.BlockSpec((N_HEADS, tq, nbp), lambda b, i: (0, b * nt + i, 0))],
        out_specs=pl.BlockSpec((tq, hq), cur),
        out_shape=jax.ShapeDtypeStruct((n, hq), f32),
        scratch_shapes=[pltpu.VMEM((2 * LANES, t), bf16), pltpu.VMEM((t, 2 * LANES), bf16),
                        pltpu.VMEM((rows, 2 * LANES), bf16), pltpu.VMEM((rows, tq), f32),
                        pltpu.VMEM((rows, tq), bf16), pltpu.VMEM((rows, LANES), f32),
                        pltpu.VMEM((rows, LANES), f32), pltpu.VMEM((rows, 2 * LANES), f32)],
        compiler_params=_params(2),
        name="block_attn_prompt",
    )(q, k, v, mask)


def _blk_sample_body(pps, nc, tq, blk_shift, *refs):
    kp = refs[1:1 + pps]
    vp = refs[1 + pps:1 + 2 * pps]
    q_ref, kn_ref, vn_ref, mpast_ref, mnew_ref, aux_ref, o_ref, qbd_scr, m_scr, acc_scr = refs[1 + 2 * pps:]
    c = pl.program_id(1)
    rows = N_HEADS * tq

    @pl.when(c == 0)
    def _():
        qbd0 = _build_qbd(q_ref[...], HEAD_DIM ** -0.5)
        qbd_scr[...] = qbd0
        tn = max(tq, LANES)
        idx = lax.broadcasted_iota(jnp.int32, (1, tn), 1)
        allowed = (_expand_mask(mnew_ref[0].astype(bf16), idx, blk_shift) & (idx <= _row_token(rows, tq, tn))
                   & (idx < tq))
        s0 = jnp.where(allowed, _dot_nt(qbd0, _pad_rows(kn_ref[...], tn).astype(bf16)), NEG_INF)
        m0 = jnp.max(s0, axis=-1, keepdims=True)
        ones = jnp.where(lax.broadcasted_iota(jnp.int32, (tn, LANES), 1) == 0, 1.0, 0.0)
        vaug_n = jnp.concatenate([_pad_rows(vn_ref[...], tn).astype(bf16), ones.astype(bf16)], axis=1)
        m_scr[...] = m0
        acc_scr[...] = _dot(jnp.exp(s0 - m0).astype(bf16), vaug_n)

    unselected = 1.0 - mpast_ref[0, 0]
    qaug = jnp.concatenate([qbd_scr[...], unselected.astype(bf16)], axis=1)
    kaug = jnp.concatenate([jnp.concatenate([p[0] for p in kp], axis=1).astype(bf16), aux_ref[0:LANES, :]], axis=0)
    s = _dot(qaug, kaug)
    m_old = m_scr[...]
    m_new = jnp.maximum(m_old, jnp.max(s, axis=-1, keepdims=True))
    p = jnp.exp(s - m_new).astype(bf16)
    vaug = jnp.concatenate([jnp.concatenate([p_[0] for p_ in vp], axis=1).astype(bf16), aux_ref[LANES:, :]], axis=0)
    acc = jnp.exp(m_old - m_new) * acc_scr[...] + _dot_nt(p, vaug)
    m_scr[...] = m_new
    acc_scr[...] = acc

    @pl.when(c == nc - 1)
    def _():
        o_ref[...] = _unstack(acc[:, :LANES] / acc[:, LANES:LANES + 1], tq)


def _block_attn_sample(q, pool_k, pool_v, table, k_new, v_new, mask, nb, tq, blk):
    n, hq = q.shape
    page = pool_k.shape[2]
    n_pages = table.shape[0] // nb
    pps = min(PAGES_PER_STEP, n_pages)
    nc = n_pages // pps
    nbp = mask.shape[2]
    rows = N_HEADS * tq
    tk = pps * page
    bpc = tk // blk
    n_past_blk = nc * bpc
    assert bpc <= LANES and nbp - n_past_blk <= LANES
    m4 = mask.reshape(N_HEADS, nb, tq, nbp)
    mpast = m4[..., :n_past_blk].reshape(N_HEADS, nb, tq, nc, bpc)
    mpast = jnp.transpose(mpast, (1, 3, 0, 2, 4)).reshape(nb, nc, rows, bpc)
    mpast = jnp.pad(mpast, ((0, 0), (0, 0), (0, 0), (0, LANES - bpc)))
    mnew = jnp.transpose(m4[..., n_past_blk:], (1, 0, 2, 3)).reshape(nb, rows, nbp - n_past_blk)
    mnew = jnp.pad(mnew, ((0, 0), (0, 0), (0, LANES - (nbp - n_past_blk))))
    col_blk = jnp.arange(tk, dtype=jnp.int32)[None, :] // blk
    row = jnp.arange(LANES, dtype=jnp.int32)[:, None]
    aux = jnp.concatenate([jnp.where(row == col_blk, NEG_INF, 0.0),
                           jnp.where(row == 0, 1.0, 0.0) * jnp.ones((1, tk), f32)], axis=0).astype(bf16)
    cur = lambda b, c, tab: (b, 0)
    grid_spec = pltpu.PrefetchScalarGridSpec(
        num_scalar_prefetch=1,
        grid=(nb, nc),
        in_specs=_page_specs(n_pages, pps, page) * 2
        + [pl.BlockSpec((tq, hq), cur), pl.BlockSpec((tq, LANES), cur), pl.BlockSpec((tq, LANES), cur),
           pl.BlockSpec((1, 1, rows, LANES), lambda b, c, tab: (b, c, 0, 0)),
           pl.BlockSpec((1, rows, LANES), lambda b, c, tab: (b, 0, 0)),
           pl.BlockSpec((2 * LANES, tk), lambda b, c, tab: (0, 0))],
        out_specs=pl.BlockSpec((tq, hq), cur),
        scratch_shapes=[pltpu.VMEM((rows, LANES), bf16), pltpu.VMEM((rows, 1), f32),
                        pltpu.VMEM((rows, 2 * LANES), f32)],
    )
    return pl.pallas_call(
        functools.partial(_blk_sample_body, pps, nc, tq, _log2(blk)),
        grid_spec=grid_spec,
        out_shape=jax.ShapeDtypeStruct((n, hq), f32),
        compiler_params=_params(2),
        name="block_attn_sample",
    )(table, *([pool_k] * pps), *([pool_v] * pps), q, k_new, v_new, mpast, mnew, aux)


def _round_up(x, m):
    return -(-x // m) * m


def _rows_last(a):
    n, rows = a.shape[:2]
    return jnp.transpose(a, (0, 2, 3, 1)).reshape(n * N_KV * HEAD_DIM, rows)


def _forward(x, nb, t, p0, past, prm):
    n = nb * t
    prompt = past is None
    tm = min(FFN_ROWS, n)
    if prompt:
        pos = jnp.arange(t, dtype=jnp.int32)
        n_tab = t // tm
    else:
        pos = p0 + jnp.arange(tm, dtype=jnp.int32) % t
        n_tab = 1
    tables = _rope_tables(pos)

    x = _ffn(x, prm["norm_ffn_a"][0], *prm["ffn_a"][0])
    cdim = prm["l0_conv_w"].shape[1]
    hq, hk = N_HEADS * HEAD_DIM, N_KV * HEAD_DIM
    gb, cu, q, k, v = _proj(functools.partial(_proj_ab_body, cdim), x, prm["norm_mix"][0], prm["l0_w_in"],
                            tables, n_tab, (cdim, cdim, hq, hk, hk))
    if prompt:
        tq = min(ATTN_ROWS, t)
        y_attn, y_conv = _window_attn(q, k, v, None, None, nb, t // tq, tq, SWA_WINDOW,
                                      conv=(gb, cu, None, prm["l0_conv_w"]), sinks=prm["l0_sinks"])
    else:
        conv_buf = jnp.pad(past["conv"], ((0, 0), (SUBLANES - (CONV_WIDTH - 1), 0), (0, 0)))
        y_attn, y_conv = _window_attn(q, k, v, _rows_last(past["swa_k"]), _rows_last(past["swa_v"]),
                                      nb, 1, t, SWA_WINDOW,
                                      conv=(gb, cu, conv_buf.reshape(nb * SUBLANES, cdim), prm["l0_conv_w"]),
                                      sinks=prm["l0_sinks"])
    x = _ffn(x, prm["norm_ffn_b"][0], *prm["ffn_b"][0], mix=(y_conv, y_attn, prm["l0_w_out"]))
    states0 = (cu, k, v)

    x = _ffn(x, prm["norm_ffn_a"][1], *prm["ffn_a"][1])
    qm, km, vm, qn, kc, vc, ks, vs, kw, vw, gates = _proj(
        _proj_cd_body, x, prm["norm_mix"][1], prm["l1_w_in"], tables, n_tab,
        (hq, hk, hk, hq, hk, hk, hk, hk, hk, hk, LANES))
    if prompt:
        tq = min(ATTN_ROWS, t)
        nt = t // tq
        n_blk = t // MOBA_BLOCK
        nbp = _round_up(n_blk + 1, LANES)
        means = _means_prompt(km, nb, t, nbp)
        moba_mask = _moba_select(qm, means, nb, nt, tq, 0, nbp)
        o_moba = _block_attn_prompt(qm, km, vm, moba_mask, nb, nt, tq, MOBA_BLOCK)
        ck = _compress_prompt(kc, nb, t, prm["cmp_k"])
        cv = _compress_prompt(vc, nb, t, prm["cmp_v"])
        n_ck = t // NSA_CMP_STRIDE
        nsp = _round_up(-(-t // NSA_SEL_BLOCK), LANES)
        o_cmp, sel_mask = _cmp_select(qn, ck, cv, nb, nt, tq, 0, n_ck, n_ck - 1, nsp)
        o_sel = _block_attn_prompt(qn, ks, vs, sel_mask, nb, nt, tq, NSA_SEL_BLOCK)
        (o_win,) = _window_attn(qn, kw, vw, None, None, nb, nt, tq, NSA_WINDOW)
    else:
        table = past["table"]
        n_past = p0
        n_blk = n_past // MOBA_BLOCK
        nbp = _round_up(n_blk + 1, LANES)
        km_t = jnp.transpose(km.reshape(nb, t, hk), (0, 2, 1)).reshape(nb * hk, t)
        means = _means_sample(past["moba_k"], table, km_t, nb, t, nbp)
        moba_mask = _moba_select(qm, means, nb, 1, t, p0, nbp)
        o_moba = _block_attn_sample(qm, past["moba_k"], past["moba_v"], table, km, vm, moba_mask, nb, t,
                                    MOBA_BLOCK)
        ck = _compress_sample(past["cmp_k"], table, kc, nb, t, prm["cmp_k"])
        cv = _compress_sample(past["cmp_v"], table, vc, nb, t, prm["cmp_v"])
        n_ck = n_past // NSA_CMP_STRIDE
        nsp = _round_up(-(-(n_past + t) // NSA_SEL_BLOCK), LANES)
        o_cmp, sel_mask = _cmp_select(qn, ck, cv, nb, 1, t, p0, n_ck, n_ck, nsp)
        o_sel = _block_attn_sample(qn, past["sel_k"], past["sel_v"], table, ks, vs, sel_mask, nb, t,
                                   NSA_SEL_BLOCK)
        (o_win,) = _window_attn(qn, kw, vw, _rows_last(past["win_k"]), _rows_last(past["win_v"]),
                                nb, 1, t, NSA_WINDOW)
    x = _ffn(x, prm["norm_ffn_b"][1], *prm["ffn_b"][1], mix=(o_moba, o_cmp, o_sel, o_win, gates, prm["l1_w_out"]),
             final_g=prm["norm_final"])
    return x, states0 + (km, vm, kc, vc, ks, vs, kw, vw)


def _tail(buf, new, rows):
    full = new if buf is None else jnp.concatenate([buf.astype(new.dtype), new], axis=1)
    return full[:, full.shape[1] - rows:]


def kernel(x_prompt, x_sample, state_l0_conv, cache_l0_swa_k, cache_l0_swa_v, cache_l1_moba_k, cache_l1_moba_v, cache_l1_nsa_cmp_k, cache_l1_nsa_cmp_v, cache_l1_nsa_sel_k, cache_l1_nsa_sel_v, cache_l1_nsa_win_k, cache_l1_nsa_win_v, page_table, norm_ffn_a, ffn_a_gate, ffn_a_up, ffn_a_down, norm_mix, norm_ffn_b, ffn_b_gate, ffn_b_up, ffn_b_down, norm_final, l0_w_in, l0_conv_w, l0_sinks, l0_w_out, l1_w_in, l1_cmp_k_w1, l1_cmp_k_b1, l1_cmp_k_w2, l1_cmp_v_w1, l1_cmp_v_b1, l1_cmp_v_w2, l1_w_out):
    bp, tp, d = x_prompt.shape
    bs, ts, _ = x_sample.shape
    depth = norm_ffn_a.shape[0]
    assert depth == 2 and N_KV * HEAD_DIM == LANES
    n_pool, page = cache_l1_moba_k.shape[:2]
    p0 = page_table.shape[1] * page
    hk = N_KV * HEAD_DIM

    w1_pad = _round_up(l1_w_in.shape[1], LANES) - l1_w_in.shape[1]
    prm = {
        "norm_ffn_a": norm_ffn_a, "norm_mix": norm_mix, "norm_ffn_b": norm_ffn_b, "norm_final": norm_final,
        "ffn_a": [(ffn_a_gate[l].astype(bf16), ffn_a_up[l].astype(bf16), ffn_a_down[l].astype(bf16))
                  for l in range(depth)],
        "ffn_b": [(ffn_b_gate[l].astype(bf16), ffn_b_up[l].astype(bf16), ffn_b_down[l].astype(bf16))
                  for l in range(depth)],
        "l0_w_in": l0_w_in.astype(bf16), "l0_conv_w": l0_conv_w, "l0_sinks": l0_sinks,
        "l0_w_out": l0_w_out.astype(bf16),
        "l1_w_in": jnp.pad(l1_w_in, ((0, 0), (0, w1_pad))).astype(bf16), "l1_w_out": l1_w_out.astype(bf16),
        "cmp_k": _cmp_weights(l1_cmp_k_w1, l1_cmp_k_b1, l1_cmp_k_w2),
        "cmp_v": _cmp_weights(l1_cmp_v_w1, l1_cmp_v_b1, l1_cmp_v_w2),
    }
    pool = lambda a: _rows_last(a).reshape(n_pool, hk, page)
    past = {
        "conv": state_l0_conv, "swa_k": cache_l0_swa_k, "swa_v": cache_l0_swa_v,
        "moba_k": pool(cache_l1_moba_k), "moba_v": pool(cache_l1_moba_v),
        "cmp_k": pool(cache_l1_nsa_cmp_k), "cmp_v": pool(cache_l1_nsa_cmp_v),
        "sel_k": pool(cache_l1_nsa_sel_k), "sel_v": pool(cache_l1_nsa_sel_v),
        "win_k": cache_l1_nsa_win_k, "win_v": cache_l1_nsa_win_v,
        "table": page_table.reshape(-1),
    }
    y_p, st_p = _forward(x_prompt.reshape(bp * tp, d), bp, tp, 0, None, prm)
    y_s, st_s = _forward(x_sample.reshape(bs * ts, d), bs, ts, p0, past, prm)

    def states(st, nb, t, bufs):
        cu, k0, v0, km, vm, kc, vc, ks, vs, kw, vw = st
        kv = lambda a: a.reshape(nb, t, N_KV, HEAD_DIM)
        conv = _tail(bufs["conv"], cu.reshape(nb, t, -1), CONV_WIDTH - 1)
        swa_rows = min(SWA_WINDOW, t) if bufs["swa_k"] is None else bufs["swa_k"].shape[1]
        win_rows = min(NSA_WINDOW, t) if bufs["win_k"] is None else bufs["win_k"].shape[1]
        return (conv, _tail(bufs["swa_k"], kv(k0), swa_rows), _tail(bufs["swa_v"], kv(v0), swa_rows),
                kv(km), kv(vm), kv(kc), kv(vc), kv(ks), kv(vs),
                _tail(bufs["win_k"], kv(kw), win_rows), _tail(bufs["win_v"], kv(vw), win_rows))

    none = {"conv": None, "swa_k": None, "swa_v": None, "win_k": None, "win_v": None}
    sp = states(st_p, bp, tp, none)
    ss = states(st_s, bs, ts, past)
    conv_p, swa_k_p, swa_v_p, mk_p, mv_p, ck_p, cv_p, sk_p, sv_p, wk_p, wv_p = sp
    conv_s, swa_k_s, swa_v_s, mk_s, mv_s, ck_s, cv_s, sk_s, sv_s, wk_s, wv_s = ss
    return (y_p.reshape(bp, tp, d), y_s.reshape(bs, ts, d), conv_p, conv_s, swa_k_p, swa_v_p, swa_k_s, swa_v_s,
            mk_p, mv_p, mk_s, mv_s, ck_p, cv_p, ck_s, cv_s,
            sk_p, sv_p, sk_s, sv_s, wk_p, wv_p, wk_s, wv_s)
```

```python
import functools
import math

import jax
import jax.numpy as jnp
from jax import lax
from jax.experimental import pallas as pl
from jax.experimental.pallas import tpu as pltpu

f32 = jnp.float32
bf16 = jnp.bfloat16

HEAD_DIM = 64
N_HEADS = 8
N_KV = 2
ROPE_THETA = 10000.0
NORM_EPS = 1e-5
CONV_WIDTH = 3
SWA_WINDOW = 128
MOBA_BLOCK = 256
MOBA_TOPK = 3
NSA_CMP_LEN = 32
NSA_CMP_STRIDE = 16
NSA_SEL_BLOCK = 64
NSA_TOPN = 16
NSA_WINDOW = 512
NEG_INF = -1e30
FORCE_SCORE = 1e9

LANES = 128
SUBLANES = 8
VMEM_LIMIT_BYTES = 60 * 1024 * 1024

FFN_ROWS = 512
PAGES_PER_STEP = 64
ATTN_ROWS = 256
SOFTMAX_ROWS = 64

_ARB = "arbitrary"


def _params(n_axes):
    return pltpu.CompilerParams(dimension_semantics=(_ARB,) * n_axes, vmem_limit_bytes=VMEM_LIMIT_BYTES)


def _log2(n):
    assert n > 0 and n & (n - 1) == 0, n
    return n.bit_length() - 1


def _dot(a, b):
    return jnp.dot(a, b, preferred_element_type=f32)


def _dot_nt(a, b):
    return lax.dot_general(a, b, (((1,), (1,)), ((), ())), preferred_element_type=f32)


def _split3(x):
    hi = x.astype(bf16)
    r1 = x - hi.astype(f32)
    mid = r1.astype(bf16)
    lo = (r1 - mid.astype(f32)).astype(bf16)
    return hi, mid, lo


def _rms(x, g):
    return x * lax.rsqrt(jnp.mean(x * x, axis=-1, keepdims=True) + NORM_EPS) * g


def _gate_expand(gates, comp):
    tm = gates.shape[0]
    head = jnp.right_shift(lax.broadcasted_iota(jnp.int32, (tm, N_HEADS * HEAD_DIM), 1), _log2(HEAD_DIM))
    out = jnp.zeros((tm, N_HEADS * HEAD_DIM), f32)
    for h in range(N_HEADS):
        col = gates[:, 3 * h + comp:3 * h + comp + 1]
        out = jnp.where(head == h, col, out)
    return out


def _ffn_body(mode, final, nf, *refs):
    it = iter(refs)
    x_ref = next(it)
    if mode == 1:
        pa_ref, pb_ref, wo_ref = next(it), next(it), next(it)
    elif mode == 2:
        om_ref, oc_ref, os_ref, ow_ref, gt_ref, wo_ref = (next(it) for _ in range(6))
    g_ref, wg_ref, wu_ref, wd_ref = next(it), next(it), next(it), next(it)
    gf_ref = next(it) if final else None
    o_ref, h_scr, acc_scr = next(it), next(it), next(it)
    j = pl.program_id(1)

    @pl.when(j == 0)
    def _():
        x1 = x_ref[...]
        if mode == 1:
            half = pa_ref.shape[1]
            x1 = x1 + _dot(pa_ref[...].astype(bf16), wo_ref[0:half, :])
            x1 = x1 + _dot(pb_ref[...].astype(bf16), wo_ref[half:2 * half, :])
        elif mode == 2:
            half = om_ref.shape[1]
            gates = gt_ref[...]
            y_nsa = (_gate_expand(gates, 0) * oc_ref[...] + _gate_expand(gates, 1) * os_ref[...]
                     + _gate_expand(gates, 2) * ow_ref[...])
            x1 = x1 + _dot(om_ref[...].astype(bf16), wo_ref[0:half, :])
            x1 = x1 + _dot(y_nsa.astype(bf16), wo_ref[half:2 * half, :])
        acc_scr[...] = x1
        h_scr[...] = _rms(x1, g_ref[...]).astype(bf16)

    h = h_scr[...]
    gate = _dot(h, wg_ref[...])
    up = _dot(h, wu_ref[...])
    act = gate * (1.0 / (1.0 + jnp.exp(-gate))) * up
    acc_scr[...] += 0.5 * _dot(act.astype(bf16), wd_ref[...])

    @pl.when(j == nf - 1)
    def _():
        y = acc_scr[...]
        if final:
            y = _rms(y, gf_ref[...])
        o_ref[...] = y


def _ffn_chunk(d_ff):
    best = LANES
    for c in range(LANES, 1536 + 1, LANES):
        if d_ff % c == 0:
            best = c
    return best if d_ff % LANES == 0 else d_ff


def _ffn(x, g, wg, wu, wd, mix=None, final_g=None):
    n, d = x.shape
    d_ff = wg.shape[1]
    tm = min(FFN_ROWS, n)
    tf = _ffn_chunk(d_ff)
    nf = d_ff // tf
    row = lambda i, j: (i, 0)
    const = lambda i, j: (0, 0)
    args, specs = [x], [pl.BlockSpec((tm, d), row)]
    mode = 0
    if mix is not None:
        mode = 1 if len(mix) == 3 else 2
        for a in mix[:-1]:
            args.append(a)
            specs.append(pl.BlockSpec((tm, a.shape[1]), row))
        args.append(mix[-1])
        specs.append(pl.BlockSpec(mix[-1].shape, const))
    args += [g.reshape(1, d), wg, wu, wd]
    specs += [pl.BlockSpec((1, d), const), pl.BlockSpec((d, tf), lambda i, j: (0, j)),
              pl.BlockSpec((d, tf), lambda i, j: (0, j)), pl.BlockSpec((tf, d), lambda i, j: (j, 0))]
    if final_g is not None:
        args.append(final_g.reshape(1, d))
        specs.append(pl.BlockSpec((1, d), const))
    return pl.pallas_call(
        functools.partial(_ffn_body, mode, final_g is not None, nf),
        grid=(n // tm, nf),
        in_specs=specs,
        out_specs=pl.BlockSpec((tm, d), row),
        out_shape=jax.ShapeDtypeStruct((n, d), f32),
        scratch_shapes=[pltpu.VMEM((tm, d), bf16), pltpu.VMEM((tm, d), f32)],
        compiler_params=_params(2),
        name="ffn",
    )(*args)


def _rope_tables(pos):
    half = HEAD_DIM // 2
    inv_freq = ROPE_THETA ** (-jnp.arange(half, dtype=f32) * 2.0 / HEAD_DIM)
    ang = pos.astype(f32)[:, None] * inv_freq[None, :]
    cos, sin = jnp.cos(ang), jnp.sin(ang)
    zero = jnp.zeros_like(sin)
    reps = LANES // HEAD_DIM
    return (jnp.concatenate([cos, cos] * reps, axis=1), jnp.concatenate([-sin, zero] * reps, axis=1),
            jnp.concatenate([zero, sin] * reps, axis=1))


def _rope(x, cos, sa, sb):
    half = HEAD_DIM // 2
    outs = []
    for c in range(x.shape[1] // LANES):
        xc = x[:, c * LANES:(c + 1) * LANES]
        outs.append(xc * cos + pltpu.roll(xc, LANES - half, 1) * sa + pltpu.roll(xc, half, 1) * sb)
    return outs[0] if len(outs) == 1 else jnp.concatenate(outs, axis=1)


def _proj_ab_body(cdim, x_ref, g_ref, w_ref, cos_ref, sa_ref, sb_ref, gb_ref, cu_ref, q_ref, k_ref, v_ref):
    h = _rms(x_ref[...], g_ref[...]).astype(bf16)
    cos, sa, sb = cos_ref[...], sa_ref[...], sb_ref[...]
    seg = lambda off, width: _dot(h, w_ref[:, off:off + width])
    hq, hk = q_ref.shape[1], k_ref.shape[1]
    gb_ref[...] = seg(0, cdim)
    cu_ref[...] = seg(cdim, cdim) * seg(2 * cdim, cdim)
    q_ref[...] = _rope(seg(3 * cdim, hq), cos, sa, sb)
    k_ref[...] = _rope(seg(3 * cdim + hq, hk), cos, sa, sb)
    v_ref[...] = seg(3 * cdim + hq + hk, hk)


def _proj_cd_body(x_ref, g_ref, w_ref, cos_ref, sa_ref, sb_ref,
                  qm_ref, km_ref, vm_ref, qn_ref, kc_ref, vc_ref, ks_ref, vs_ref, kw_ref, vw_ref, gt_ref):
    h = _rms(x_ref[...], g_ref[...]).astype(bf16)
    cos, sa, sb = cos_ref[...], sa_ref[...], sb_ref[...]
    off = 0
    for ref, kind in ((qm_ref, "rope"), (km_ref, "rope"), (vm_ref, ""), (qn_ref, "rope"), (kc_ref, ""),
                      (vc_ref, ""), (ks_ref, "rope"), (vs_ref, ""), (kw_ref, "rope"), (vw_ref, ""),
                      (gt_ref, "sigmoid")):
        width = ref.shape[1]
        z = _dot(h, w_ref[:, off:off + width])
        if kind == "rope":
            z = _rope(z, cos, sa, sb)
        elif kind == "sigmoid":
            z = 1.0 / (1.0 + jnp.exp(-z))
        ref[...] = z
        off += width


def _proj(body, x, g, w, tables, n_tab_tiles, widths):
    n, d = x.shape
    tm = min(FFN_ROWS, n)
    row = lambda i: (i, 0)
    const = lambda i: (0, 0)
    tab = lambda i: (i % n_tab_tiles, 0)
    return pl.pallas_call(
        body,
        grid=(n // tm,),
        in_specs=[pl.BlockSpec((tm, d), row), pl.BlockSpec((1, d), const), pl.BlockSpec(w.shape, const)]
        + [pl.BlockSpec((tm, LANES), tab)] * 3,
        out_specs=[pl.BlockSpec((tm, wd), row) for wd in widths],
        out_shape=[jax.ShapeDtypeStruct((n, wd), f32) for wd in widths],
        compiler_params=_params(1),
        name="proj",
    )(x, g.reshape(1, d), w, *tables)


def _build_qbd(q, scale):
    tq = q.shape[0]
    lo = lax.broadcasted_iota(jnp.int32, (tq, LANES), 1) < HEAD_DIM
    per = N_HEADS // N_KV
    pieces = []
    for h in range(N_HEADS):
        sl = q[:, LANES * (h // 2):LANES * (h // 2) + LANES]
        src_hi, dst_hi = (h % 2 == 1), (h // per == 1)
        if src_hi != dst_hi:
            sl = pltpu.roll(sl, HEAD_DIM, 1)
        pieces.append(jnp.where(lo != dst_hi, sl, 0.0))
    return (jnp.concatenate(pieces, axis=0) * scale).astype(bf16)


def _unstack(o, tq):
    lo = lax.broadcasted_iota(jnp.int32, (tq, LANES), 1) < HEAD_DIM
    per = N_HEADS // N_KV
    outs = []
    for c in range(N_HEADS // 2):
        a = o[(2 * c) * tq:(2 * c + 1) * tq]
        b = o[(2 * c + 1) * tq:(2 * c + 2) * tq]
        if (2 * c) // per == 1:
            a = pltpu.roll(a, HEAD_DIM, 1)
        if (2 * c + 1) // per == 0:
            b = pltpu.roll(b, HEAD_DIM, 1)
        outs.append(jnp.where(lo, a, b))
    return jnp.concatenate(outs, axis=1)


def _row_token(rows, tq, width):
    _log2(tq)
    return lax.broadcasted_iota(jnp.int32, (rows, width), 0) & (tq - 1)


def _attn_update(qbd, k, v, allowed, m, l, acc, transposed=False):
    k, v = k.astype(bf16), v.astype(bf16)
    s = jnp.where(allowed, _dot(qbd, k) if transposed else _dot_nt(qbd, k), NEG_INF)
    m_new = jnp.maximum(m, jnp.max(s, axis=-1, keepdims=True))
    alpha = jnp.exp(m - m_new)
    p = jnp.where(allowed, jnp.exp(s - m_new), 0.0)
    l_new = alpha * l + jnp.sum(p, axis=-1, keepdims=True)
    p = p.astype(bf16)
    acc_new = alpha * acc + (_dot_nt(p, v) if transposed else _dot(p, v))
    return m_new, l_new, acc_new


def _pad_rows(x, rows):
    if x.shape[0] >= rows:
        return x
    return jnp.concatenate([x, jnp.zeros((rows - x.shape[0], x.shape[1]), x.dtype)], axis=0)


def _expand_mask(m_rows, key_pos, blk_shift):
    nbp, tk = m_rows.shape[1], key_pos.shape[1]
    blk = jnp.right_shift(key_pos, blk_shift)
    e = jnp.where(lax.broadcasted_iota(jnp.int32, (nbp, tk), 0) == blk, 1.0, 0.0).astype(bf16)
    return _dot(m_rows, e) > 0.5


def _win_body(tq, window, hb, nh, hist_always, with_conv, with_sinks, *refs):
    it = iter(refs)
    q_ref, kc_ref, vc_ref = (next(it) for _ in range(3))
    kh_refs = [next(it) for _ in range(nh)]
    vh_refs = [next(it) for _ in range(nh)]
    if with_conv:
        gb_ref, cu_ref, cup_ref, cw_ref = (next(it) for _ in range(4))
    sink_ref = next(it) if with_sinks else None
    o_ref = next(it)
    oc_ref = next(it) if with_conv else None
    i = pl.program_id(1)
    rows = N_HEADS * tq

    qbd = _build_qbd(q_ref[...], HEAD_DIM ** -0.5)
    m = jnp.full((rows, 1), NEG_INF, f32)
    l = jnp.zeros((rows, 1), f32)
    acc = jnp.zeros((rows, LANES), f32)
    t_h = _row_token(rows, tq, hb)
    c_h = lax.broadcasted_iota(jnp.int32, (rows, hb), 1)
    for j in range(nh):
        ok = (c_h - (nh - j) * hb >= t_h - window)
        if not hist_always:
            ok = ok & (i * (tq // hb) >= nh - j)
        m, l, acc = _attn_update(qbd, kh_refs[j][...], vh_refs[j][...], ok, m, l, acc, transposed=hist_always)
    hist_ok = True if hist_always else (i > 0)
    tk = max(tq, LANES)
    t_c = _row_token(rows, tq, tk)
    c_c = lax.broadcasted_iota(jnp.int32, (rows, tk), 1)
    allowed = (c_c <= t_c) & (c_c >= t_c - window) & (c_c < tq)
    m, l, acc = _attn_update(qbd, _pad_rows(kc_ref[...], tk), _pad_rows(vc_ref[...], tk), allowed, m, l, acc)
    if with_sinks:
        head = jnp.right_shift(lax.broadcasted_iota(jnp.int32, (rows, 1), 0), _log2(tq))
        sink = jnp.zeros((rows, 1), f32)
        for h in range(N_HEADS):
            sink = jnp.where(head == h, sink_ref[h], sink)
        m_f = jnp.maximum(m, sink)
        scale = jnp.exp(m - m_f)
        out = acc * scale / (l * scale + jnp.exp(sink - m_f))
    else:
        out = acc / l
    o_ref[...] = _unstack(out, tq)

    if with_conv:
        cu = cu_ref[...]
        prev = jnp.where(hist_ok, cup_ref[...], 0.0)
        r = lax.broadcasted_iota(jnp.int32, cu.shape, 0)
        p1 = prev[SUBLANES - 1:SUBLANES, :]
        p2 = prev[SUBLANES - 2:SUBLANES - 1, :]
        cu_m1 = jnp.where(r == 0, p1, pltpu.roll(cu, 1, 0))
        cu_m2 = jnp.where(r == 0, p2, jnp.where(r == 1, p1, pltpu.roll(cu, 2, 0)))
        cw = cw_ref[...]
        oc_ref[...] = gb_ref[...] * (cu_m2 * cw[0:1, :] + cu_m1 * cw[1:2, :] + cu * cw[2:3, :])


def _window_attn(q, k, v, k_hist, v_hist, nb, nt, tq, window, conv=None, sinks=None):
    n = q.shape[0]
    hq = q.shape[1]
    per_seq = nt * tq
    cur = lambda b, i: (b * nt + i, 0)
    if k_hist is None:
        hist_always = False
        hb = min(window, tq)
        nh = window // hb
        assert tq % hb == 0 and window % hb == 0

        def hist_map(j):
            return lambda b, i: (b * (per_seq // hb) + jnp.maximum(i * (tq // hb) - (nh - j), 0), 0)

        k_hist, v_hist = k, v
    else:
        hist_always = True
        hb, nh = window, 1
        hist_map = lambda j: (lambda b, i: (b, 0))
    hist_block = (LANES, hb) if hist_always else (hb, LANES)
    args = [q, k, v] + [k_hist] * nh + [v_hist] * nh
    specs = ([pl.BlockSpec((tq, hq), cur), pl.BlockSpec((tq, LANES), cur), pl.BlockSpec((tq, LANES), cur)]
             + [pl.BlockSpec(hist_block, hist_map(j)) for j in range(nh)] * 2)
    out_shape = [jax.ShapeDtypeStruct((n, hq), f32)]
    out_specs = [pl.BlockSpec((tq, hq), cur)]
    if conv is not None:
        gb, cu, cu_prev, cw = conv
        cdim = gb.shape[1]
        if cu_prev is None:
            prev = lambda b, i: (jnp.maximum((b * per_seq + i * tq) // SUBLANES - 1, 0), 0)
            cu_prev = cu
        else:
            prev = lambda b, i: (b, 0)
        args += [gb, cu, cu_prev, cw]
        specs += [pl.BlockSpec((tq, cdim), cur), pl.BlockSpec((tq, cdim), cur),
                  pl.BlockSpec((SUBLANES, cdim), prev), pl.BlockSpec(cw.shape, lambda b, i: (0, 0))]
        out_shape.append(jax.ShapeDtypeStruct((n, cdim), f32))
        out_specs.append(pl.BlockSpec((tq, cdim), cur))
    if sinks is not None:
        args.append(sinks)
        specs.append(pl.BlockSpec(memory_space=pltpu.SMEM))
    return pl.pallas_call(
        functools.partial(_win_body, tq, window, hb, nh, hist_always, conv is not None, sinks is not None),
        grid=(nb, nt),
        in_specs=specs,
        out_specs=out_specs,
        out_shape=out_shape,
        compiler_params=_params(2),
        name="window_attn",
    )(*args)


def _cmp_weights(w1, b1, w2):
    n_half = NSA_CMP_LEN // NSA_CMP_STRIDE
    hid = w1.shape[2]
    w1h = w1.reshape(n_half, NSA_CMP_STRIDE, HEAD_DIM, hid)
    big = jnp.zeros((NSA_CMP_STRIDE, N_KV * HEAD_DIM, n_half * N_KV * hid), f32)
    for hh in range(n_half):
        for g in range(N_KV):
            big = big.at[:, g * HEAD_DIM:(g + 1) * HEAD_DIM,
                         hh * N_KV * hid + g * hid:hh * N_KV * hid + (g + 1) * hid].set(w1h[hh])
    w2bd = jnp.zeros((N_KV * hid, N_KV * HEAD_DIM), f32)
    for g in range(N_KV):
        w2bd = w2bd.at[g * hid:(g + 1) * hid, g * HEAD_DIM:(g + 1) * HEAD_DIM].set(w2)
    return big.astype(bf16), jnp.tile(b1, N_KV).reshape(1, N_KV * hid), w2bd.astype(bf16)


def _cmp_finish(hp, n_tok, b1, w2bd):
    rows = hp.shape[0]
    hp1 = pltpu.roll(hp[:, LANES:], rows - 1, 0)
    hid = jax.nn.gelu(hp[:, :LANES] + hp1 + b1)
    return _dot(hid.astype(bf16), w2bd)[:n_tok]


def _cmp_prompt_body(n_chunk, x_ref, w1_ref, b1_ref, w2_ref, o_ref):
    hp = jnp.zeros((n_chunk, 2 * LANES), f32)
    for s in range(NSA_CMP_STRIDE):
        hp = hp + _dot(x_ref[pl.ds(s, n_chunk, stride=NSA_CMP_STRIDE), :].astype(bf16), w1_ref[s])
    r = lax.broadcasted_iota(jnp.int32, (n_chunk, LANES), 0)
    hp1 = jnp.where(r == n_chunk - 1, 0.0, pltpu.roll(hp[:, LANES:], n_chunk - 1, 0))
    hid = jax.nn.gelu(hp[:, :LANES] + hp1 + b1_ref[...])
    o_ref[...] = _dot(hid.astype(bf16), w2_ref[...])


def _compress_prompt(x, nb, t, wts):
    w1, b1, w2 = wts
    n_chunk = t // NSA_CMP_STRIDE
    c3 = lambda b: (0, 0, 0)
    c2 = lambda b: (0, 0)
    return pl.pallas_call(
        functools.partial(_cmp_prompt_body, n_chunk),
        grid=(nb,),
        in_specs=[pl.BlockSpec((t, LANES), lambda b: (b, 0)), pl.BlockSpec(w1.shape, c3),
                  pl.BlockSpec(b1.shape, c2), pl.BlockSpec(w2.shape, c2)],
        out_specs=pl.BlockSpec((n_chunk, LANES), lambda b: (b, 0)),
        out_shape=jax.ShapeDtypeStruct((nb * n_chunk, LANES), f32),
        compiler_params=_params(1),
        name="compress_prompt",
    )(x, w1, b1, w2)


def _page_specs(n_pages, pps, page):
    specs = []
    for i in range(pps):
        specs.append(pl.BlockSpec((1, LANES, page),
                                  lambda b, c, tab, i=i: (tab[b * n_pages + c * pps + i], 0, 0)))
    return specs


def _cmp_sample_body(pps, nc, page, n_past_chunk, *refs):
    tab_ref = refs[0]
    pages = refs[1:1 + pps]
    xn_ref, w1_ref, b1_ref, w2_ref, o_ref, hp_scr = refs[1 + pps:]
    del tab_ref
    c = pl.program_id(1)
    cpp = page // NSA_CMP_STRIDE
    out_r = lax.broadcasted_iota(jnp.int32, (page, page), 0)
    src_r = lax.broadcasted_iota(jnp.int32, (page, page), 1)
    src_of = (out_r & (cpp - 1)) * NSA_CMP_STRIDE + jnp.right_shift(out_r, _log2(cpp))
    perm = jnp.where(src_r == src_of, 1.0, 0.0).astype(bf16)
    regrouped = [_dot_nt(perm, p[0].astype(bf16)) for p in pages]
    flat = jnp.concatenate([jnp.concatenate([x[s * cpp:(s + 1) * cpp] for x in regrouped], axis=0)
                            for s in range(NSA_CMP_STRIDE)], axis=1)
    hp = _dot(flat.astype(bf16), w1_ref[...].reshape(NSA_CMP_STRIDE * LANES, 2 * LANES))
    hp_scr[pl.ds(pl.multiple_of(c * (pps * cpp), SUBLANES), pps * cpp), :] = hp

    @pl.when(c == nc - 1)
    def _():
        xn = xn_ref[...].astype(bf16)
        r = lax.broadcasted_iota(jnp.int32, (NSA_CMP_STRIDE, 2 * LANES), 0)
        new = jnp.zeros((NSA_CMP_STRIDE, 2 * LANES), f32)
        for s in range(NSA_CMP_STRIDE):
            new = new + jnp.where(r == s, _dot(xn, w1_ref[s]), 0.0)
        new = jnp.sum(new, axis=0, keepdims=True)
        r8 = lax.broadcasted_iota(jnp.int32, (SUBLANES, 2 * LANES), 0)
        hp_scr[n_past_chunk:n_past_chunk + SUBLANES, :] = jnp.where(r8 == 0, new, 0.0)
        o_ref[...] = _cmp_finish(hp_scr[...], n_past_chunk, b1_ref[...], w2_ref[...])


def _compress_sample(pool, table, x_new, nb, t_new, wts):
    w1, b1, w2 = wts
    page = pool.shape[2]
    n_pages = table.shape[0] // nb
    pps = min(PAGES_PER_STEP, n_pages)
    nc = n_pages // pps
    n_past_chunk = n_pages * page // NSA_CMP_STRIDE
    assert t_new <= NSA_CMP_STRIDE
    xn = jnp.pad(x_new.reshape(nb, t_new, LANES), ((0, 0), (0, NSA_CMP_STRIDE - t_new), (0, 0)))
    xn = xn.reshape(nb * NSA_CMP_STRIDE, LANES)
    c3 = lambda b, c, tab: (0, 0, 0)
    c2 = lambda b, c, tab: (0, 0)
    grid_spec = pltpu.PrefetchScalarGridSpec(
        num_scalar_prefetch=1,
        grid=(nb, nc),
        in_specs=_page_specs(n_pages, pps, page)
        + [pl.BlockSpec((NSA_CMP_STRIDE, LANES), lambda b, c, tab: (b, 0)), pl.BlockSpec(w1.shape, c3),
           pl.BlockSpec(b1.shape, c2), pl.BlockSpec(w2.shape, c2)],
        out_specs=pl.BlockSpec((n_past_chunk, LANES), lambda b, c, tab: (b, 0)),
        scratch_shapes=[pltpu.VMEM((n_past_chunk + SUBLANES, 2 * LANES), f32)],
    )
    return pl.pallas_call(
        functools.partial(_cmp_sample_body, pps, nc, page, n_past_chunk),
        grid_spec=grid_spec,
        out_shape=jax.ShapeDtypeStruct((nb * n_past_chunk, LANES), f32),
        compiler_params=_params(2),
        name="compress_sample",
    )(table, *([pool] * pps), xn, w1, b1, w2)


def _means_prompt_body(n_blk, nbp, k_ref, o_ref):
    t = k_ref.shape[0]
    sums = jnp.sum(k_ref[...].reshape(n_blk, t // n_blk, LANES), axis=1)
    o_ref[...] = _pad_rows(sums * (1.0 / MOBA_BLOCK), nbp).T


def _means_prompt(k, nb, t, nbp):
    n_blk = t // MOBA_BLOCK
    return pl.pallas_call(
        functools.partial(_means_prompt_body, n_blk, nbp),
        grid=(nb,),
        in_specs=[pl.BlockSpec((t, LANES), lambda b: (b, 0))],
        out_specs=pl.BlockSpec((LANES, nbp), lambda b: (b, 0)),
        out_shape=jax.ShapeDtypeStruct((nb * LANES, nbp), f32),
        compiler_params=_params(1),
        name="moba_means_prompt",
    )(k)


def _means_sample_body(pps, nc, n_past_blk, nbp, *refs):
    pages = refs[1:1 + pps]
    kn_ref, o_ref = refs[1 + pps:]
    c = pl.program_id(1)
    per = pps // 2

    lane = lax.broadcasted_iota(jnp.int32, (LANES, nbp), 1)
    @pl.when(c == 0)
    def _():
        o_ref[...] = jnp.zeros((LANES, nbp), f32)

    acc = o_ref[...]
    for i in range(per):
        col = jnp.sum(pages[2 * i][0] + pages[2 * i + 1][0], axis=1, keepdims=True) * (1.0 / MOBA_BLOCK)
        acc = jnp.where(lane == c * per + i, col, acc)
    new = jnp.sum(kn_ref[...], axis=1, keepdims=True) * (1.0 / MOBA_BLOCK)
    acc = jnp.where((lane == n_past_blk) & (c == nc - 1), new, acc)
    o_ref[...] = acc


def _means_sample(pool, table, k_new_t, nb, t_new, nbp):
    page = pool.shape[2]
    n_pages = table.shape[0] // nb
    pps = min(PAGES_PER_STEP, n_pages)
    nc = n_pages // pps
    assert MOBA_BLOCK == 2 * page and t_new <= MOBA_BLOCK and pps % 2 == 0
    n_past_blk = n_pages * page // MOBA_BLOCK
    grid_spec = pltpu.PrefetchScalarGridSpec(
        num_scalar_prefetch=1,
        grid=(nb, nc),
        in_specs=_page_specs(n_pages, pps, page) + [pl.BlockSpec((LANES, t_new), lambda b, c, tab: (b, 0))],
        out_specs=pl.BlockSpec((LANES, nbp), lambda b, c, tab: (b, 0)),
    )
    return pl.pallas_call(
        functools.partial(_means_sample_body, pps, nc, n_past_blk, nbp),
        grid_spec=grid_spec,
        out_shape=jax.ShapeDtypeStruct((nb * LANES, nbp), f32),
        compiler_params=_params(2),
        name="moba_means_sample",
    )(table, *([pool] * pps), k_new_t)


def _topk_cols(score, k):
    n = score.shape[0]
    ridx = lax.broadcasted_iota(jnp.int32, score.shape, 0)
    sel = jnp.zeros(score.shape, f32)
    work = score
    for _ in range(k):
        mx = jnp.max(work, axis=0, keepdims=True)
        cand = (work == mx) & (mx > -jnp.inf)
        first = jnp.min(jnp.where(cand, ridx, n), axis=0, keepdims=True)
        pick = ridx == first
        sel = jnp.where(pick, 1.0, sel)
        work = jnp.where(pick, -jnp.inf, work)
    return sel


def _moba_sel_body(tq, p0, nbp, q_ref, mean_ref, o_ref):
    i = pl.program_id(1)
    rows = N_HEADS * tq
    cols = max(rows, LANES)
    qbd = _pad_rows(_build_qbd(q_ref[...], 1.0).astype(f32), cols)
    q_hi = qbd.astype(bf16)
    q_lo = (qbd - q_hi.astype(f32)).astype(bf16)
    mean = mean_ref[...]
    m_hi = mean.astype(bf16)
    m_lo = (mean - m_hi.astype(f32)).astype(bf16)
    s = (_dot(q_hi, m_hi) + _dot(q_lo, m_hi) + _dot(q_hi, m_lo)).T
    n_idx = lax.broadcasted_iota(jnp.int32, (nbp, cols), 0)
    t = lax.broadcasted_iota(jnp.int32, (nbp, cols), 1) & (tq - 1)
    cur = jnp.right_shift(p0 + i * tq + t, _log2(MOBA_BLOCK))
    sel = _topk_cols(jnp.where(n_idx < cur, s, -jnp.inf), MOBA_TOPK)
    sel = jnp.where(n_idx == cur, 1.0, sel)
    sel_t = sel.T
    for h in range(N_HEADS):
        o_ref[h] = sel_t[h * tq:(h + 1) * tq]


def _moba_select(q, means, nb, nt, tq, p0, nbp):
    n, hq = q.shape
    return pl.pallas_call(
        functools.partial(_moba_sel_body, tq, p0, nbp),
        grid=(nb, nt),
        in_specs=[pl.BlockSpec((tq, hq), lambda b, i: (b * nt + i, 0)),
                  pl.BlockSpec((LANES, nbp), lambda b, i: (b, 0))],
        out_specs=pl.BlockSpec((N_HEADS, tq, nbp), lambda b, i: (0, b * nt + i, 0)),
        out_shape=jax.ShapeDtypeStruct((N_HEADS, n, nbp), f32),
        compiler_params=_params(2),
        name="moba_select",
    )(q, means)


def _cmp_sel_body(tq, p0, n_ck, n_cmp, nsp, q_ref, ck_ref, cv_ref, o_ref, m_ref):
    i = pl.program_id(1)
    rows = N_HEADS * tq
    per = N_HEADS // N_KV
    qbd = _build_qbd(q_ref[...], HEAD_DIM ** -0.5)
    n_idx = lax.broadcasted_iota(jnp.int32, (rows, n_ck), 1)
    pos = p0 + i * tq + _row_token(rows, tq, n_ck)
    valid = (n_idx * NSA_CMP_STRIDE + (NSA_CMP_LEN - 1) <= pos) & (n_idx < n_cmp)
    s = jnp.where(valid, _dot_nt(qbd, ck_ref[...].astype(bf16)), NEG_INF)
    mx = jnp.max(s, axis=-1, keepdims=True)
    e = jnp.where(valid, jnp.exp(s - mx), 0.0)
    den = jnp.sum(e, axis=-1, keepdims=True)
    p = e * jnp.where(den > 0.0, 1.0 / den, 0.0)
    o_ref[...] = _unstack(_dot(p.astype(bf16), cv_ref[...].astype(bf16)), tq)

    cols = max(tq, LANES)
    j_idx = lax.broadcasted_iota(jnp.int32, (nsp, n_ck), 0)
    n_of = lax.broadcasted_iota(jnp.int32, (nsp, n_ck), 1)
    ratio = NSA_SEL_BLOCK // NSA_CMP_STRIDE
    m_span = NSA_CMP_LEN // NSA_CMP_STRIDE
    lo = ratio * j_idx - (m_span - 1)
    a_t = jnp.where((n_of >= lo) & (n_of <= lo + ratio + m_span - 2), 1.0, 0.0).astype(bf16)
    j_col = lax.broadcasted_iota(jnp.int32, (nsp, N_KV * cols), 0)
    t_col = lax.broadcasted_iota(jnp.int32, (nsp, N_KV * cols), 1) & (cols - 1)
    j_cur = jnp.right_shift(p0 + i * tq + t_col, _log2(NSA_SEL_BLOCK))
    forced = (j_col == 0) | (j_col == j_cur) | (j_col == j_cur - 1)
    p_slc = []
    for g in range(N_KV):
        imp = p[(g * per) * tq:(g * per + 1) * tq]
        for r in range(1, per):
            imp = imp + p[(g * per + r) * tq:(g * per + r + 1) * tq]
        imp = _pad_rows(imp, cols)
        hi, mid, lo3 = _split3(imp)
        p_slc.append(_dot_nt(a_t, hi) + _dot_nt(a_t, mid) + _dot_nt(a_t, lo3))
    score = jnp.where(forced, FORCE_SCORE, jnp.where(j_col <= j_cur, jnp.concatenate(p_slc, axis=1), -jnp.inf))
    sel = _topk_cols(score, NSA_TOPN)
    for g in range(N_KV):
        sel_t = sel[:, g * cols:(g + 1) * cols].T[:tq]
        for r in range(per):
            m_ref[g * per + r] = sel_t


def _cmp_select(q, ck, cv, nb, nt, tq, p0, n_ck, n_cmp, nsp):
    n, hq = q.shape
    cur = lambda b, i: (b * nt + i, 0)
    return pl.pallas_call(
        functools.partial(_cmp_sel_body, tq, p0, n_ck, n_cmp, nsp),
        grid=(nb, nt),
        in_specs=[pl.BlockSpec((tq, hq), cur), pl.BlockSpec((n_ck, LANES), lambda b, i: (b, 0)),
                  pl.BlockSpec((n_ck, LANES), lambda b, i: (b, 0))],
        out_specs=[pl.BlockSpec((tq, hq), cur),
                   pl.BlockSpec((N_HEADS, tq, nsp), lambda b, i: (0, b * nt + i, 0))],
        out_shape=[jax.ShapeDtypeStruct((n, hq), f32), jax.ShapeDtypeStruct((N_HEADS, n, nsp), f32)],
        compiler_params=_params(2),
        name="cmp_select",
    )(q, ck, cv)


def _blk_prompt_body(tq, blk_shift, q_ref, k_ref, v_ref, mask_ref, o_ref,
                     kaug_scr, vaug_scr, qaug_scr, s_scr, p_scr, a_scr, m_scr, acc_scr):
    i = pl.program_id(1)
    rows = N_HEADS * tq
    nbp = mask_ref.shape[2]
    t_len = k_ref.shape[0]

    @pl.when(i == 0)
    def _():
        key = lax.broadcasted_iota(jnp.int32, (nbp, t_len), 1)
        blk = lax.broadcasted_iota(jnp.int32, (nbp, t_len), 0)
        kaug_scr[:LANES, :] = k_ref[...].T.astype(bf16)
        kaug_scr[LANES:, :] = jnp.where(blk == jnp.right_shift(key, blk_shift), NEG_INF, 0.0).astype(bf16)
        ones = jnp.where(lax.broadcasted_iota(jnp.int32, (t_len, LANES), 1) == 0, 1.0, 0.0)
        vaug_scr[:, :LANES] = v_ref[...].astype(bf16)
        vaug_scr[:, LANES:] = ones.astype(bf16)

    unselected = 1.0 - mask_ref[...].reshape(rows, nbp)
    qbd = _build_qbd(q_ref[...], HEAD_DIM ** -0.5 * math.log2(math.e))
    qaug_scr[...] = jnp.concatenate([qbd, unselected.astype(bf16)], axis=1)
    m_scr[...] = jnp.full((rows, LANES), NEG_INF, f32)
    acc_scr[...] = jnp.zeros((rows, 2 * LANES), f32)

    def tile(j, causal):
        st = pl.multiple_of(j * tq, tq)
        s_scr[...] = _dot(qaug_scr[...], kaug_scr[:, pl.ds(st, tq)])
        for r in range(rows // SOFTMAX_ROWS):
            sl = slice(r * SOFTMAX_ROWS, (r + 1) * SOFTMAX_ROWS)
            s = s_scr[sl, :]
            if causal:
                t_row = (r * SOFTMAX_ROWS + lax.broadcasted_iota(jnp.int32, (SOFTMAX_ROWS, tq), 0)) & (tq - 1)
                s = jnp.where(lax.broadcasted_iota(jnp.int32, (SOFTMAX_ROWS, tq), 1) <= t_row, s, NEG_INF)
            m_old = m_scr[sl, :]
            m_new = jnp.maximum(m_old, jnp.max(s, axis=-1, keepdims=True))
            p_scr[sl, :] = jnp.exp2(s - jnp.concatenate([m_new] * (tq // LANES), axis=1)).astype(bf16)
            a_scr[sl, :] = jnp.exp2(m_old - m_new)
            m_scr[sl, :] = m_new
        a = a_scr[...]
        acc_scr[...] = jnp.concatenate([a, a], axis=1) * acc_scr[...] + _dot(p_scr[...], vaug_scr[pl.ds(st, tq), :])

    tile(i, True)
    lax.fori_loop(0, i, lambda j, carry: (tile(j, False), carry)[1], 0)
    acc = acc_scr[...]
    o_ref[...] = _unstack(acc[:, :LANES] / acc[:, LANES:LANES + 1], tq)


def _block_attn_prompt(q, k, v, mask, nb, nt, tq, blk):
    n, hq = q.shape
    t = nt * tq
    nbp = mask.shape[2]
    assert nbp == LANES
    rows = N_HEADS * tq
    cur = lambda b, i: (b * nt + i, 0)
    seq = lambda b, i: (b, 0)
    return pl.pallas_call(
        functools.partial(_blk_prompt_body, tq, _log2(blk)),
        grid=(nb, nt),
        in_specs=[pl.BlockSpec((tq, hq), cur), pl.BlockSpec((t, LANES), seq), pl.BlockSpec((t, LANES), seq),
                  pl.BlockSpec((N_HEADS, tq, nbp), lambda b, i: (0, b * nt + i, 0))],
        out_specs=pl.BlockSpec((tq, hq), cur),
        out_shape=jax.ShapeDtypeStruct((n, hq), f32),
        scratch_shapes=[pltpu.VMEM((2 * LANES, t), bf16), pltpu.VMEM((t, 2 * LANES), bf16),
                        pltpu.VMEM((rows, 2 * LANES), bf16), pltpu.VMEM((rows, tq), f32),
                        pltpu.VMEM((rows, tq), bf16), pltpu.VMEM((rows, LANES), f32),
                        pltpu.VMEM((rows, LANES), f32), pltpu.VMEM((rows, 2 * LANES), f32)],
        compiler_params=_params(2),
        name="block_attn_prompt",
    )(q, k, v, mask)


def _blk_sample_body(pps, nc, tq, blk_shift, *refs):
    kp = refs[1:1 + pps]
    vp = refs[1 + pps:1 + 2 * pps]
    q_ref, kn_ref, vn_ref, mpast_ref, mnew_ref, aux_ref, o_ref, qbd_scr, m_scr, acc_scr = refs[1 + 2 * pps:]
    c = pl.program_id(1)
    rows = N_HEADS * tq

    @pl.when(c == 0)
    def _():
        qbd0 = _build_qbd(q_ref[...], HEAD_DIM ** -0.5)
        qbd_scr[...] = qbd0
        tn = max(tq, LANES)
        idx = lax.broadcasted_iota(jnp.int32, (1, tn), 1)
        allowed = (_expand_mask(mnew_ref[0].astype(bf16), idx, blk_shift) & (idx <= _row_token(rows, tq, tn))
                   & (idx < tq))
        s0 = jnp.where(allowed, _dot_nt(qbd0, _pad_rows(kn_ref[...], tn).astype(bf16)), NEG_INF)
        m0 = jnp.max(s0, axis=-1, keepdims=True)
        ones = jnp.where(lax.broadcasted_iota(jnp.int32, (tn, LANES), 1) == 0, 1.0, 0.0)
        vaug_n = jnp.concatenate([_pad_rows(vn_ref[...], tn).astype(bf16), ones.astype(bf16)], axis=1)
        m_scr[...] = m0
        acc_scr[...] = _dot(jnp.exp(s0 - m0).astype(bf16), vaug_n)

    unselected = (1.0 - mpast_ref[0, 0]).astype(bf16)
    k_t = jnp.concatenate([p[0] for p in kp], axis=1).astype(bf16)
    s = _dot(qbd_scr[...], k_t) + _dot(unselected, aux_ref[0:LANES, :])
    m_old = m_scr[...]
    m_new = jnp.maximum(m_old, jnp.max(s, axis=-1, keepdims=True))
    p = jnp.exp(s - m_new).astype(bf16)
    v_t = jnp.concatenate([p_[0] for p_ in vp], axis=1).astype(bf16)
    pv = jnp.concatenate([_dot_nt(p, v_t), _dot_nt(p, aux_ref[LANES:, :])], axis=1)
    acc = jnp.exp(m_old - m_new) * acc_scr[...] + pv
    m_scr[...] = m_new
    acc_scr[...] = acc

    @pl.when(c == nc - 1)
    def _():
        o_ref[...] = _unstack(acc[:, :LANES] / acc[:, LANES:LANES + 1], tq)


def _block_attn_sample(q, pool_k, pool_v, table, k_new, v_new, mask, nb, tq, blk):
    n, hq = q.shape
    page = pool_k.shape[2]
    n_pages = table.shape[0] // nb
    pps = min(PAGES_PER_STEP, n_pages)
    nc = n_pages // pps
    nbp = mask.shape[2]
    rows = N_HEADS * tq
    tk = pps * page
    bpc = tk // blk
    n_past_blk = nc * bpc
    assert bpc <= LANES and nbp - n_past_blk <= LANES
    m4 = mask.reshape(N_HEADS, nb, tq, nbp)
    mpast = m4[..., :n_past_blk].reshape(N_HEADS, nb, tq, nc, bpc)
    mpast = jnp.transpose(mpast, (1, 3, 0, 2, 4)).reshape(nb, nc, rows, bpc)
    mpast = jnp.pad(mpast, ((0, 0), (0, 0), (0, 0), (0, LANES - bpc)))
    mnew = jnp.transpose(m4[..., n_past_blk:], (1, 0, 2, 3)).reshape(nb, rows, nbp - n_past_blk)
    mnew = jnp.pad(mnew, ((0, 0), (0, 0), (0, LANES - (nbp - n_past_blk))))
    col_blk = jnp.arange(tk, dtype=jnp.int32)[None, :] // blk
    row = jnp.arange(LANES, dtype=jnp.int32)[:, None]
    aux = jnp.concatenate([jnp.where(row == col_blk, NEG_INF, 0.0),
                           jnp.where(row == 0, 1.0, 0.0) * jnp.ones((1, tk), f32)], axis=0).astype(bf16)
    cur = lambda b, c, tab: (b, 0)
    grid_spec = pltpu.PrefetchScalarGridSpec(
        num_scalar_prefetch=1,
        grid=(nb, nc),
        in_specs=_page_specs(n_pages, pps, page) * 2
        + [pl.BlockSpec((tq, hq), cur), pl.BlockSpec((tq, LANES), cur), pl.BlockSpec((tq, LANES), cur),
           pl.BlockSpec((1, 1, rows, LANES), lambda b, c, tab: (b, c, 0, 0)),
           pl.BlockSpec((1, rows, LANES), lambda b, c, tab: (b, 0, 0)),
           pl.BlockSpec((2 * LANES, tk), lambda b, c, tab: (0, 0))],
        out_specs=pl.BlockSpec((tq, hq), cur),
        scratch_shapes=[pltpu.VMEM((rows, LANES), bf16), pltpu.VMEM((rows, 1), f32),
                        pltpu.VMEM((rows, 2 * LANES), f32)],
    )
    return pl.pallas_call(
        functools.partial(_blk_sample_body, pps, nc, tq, _log2(blk)),
        grid_spec=grid_spec,
        out_shape=jax.ShapeDtypeStruct((n, hq), f32),
        compiler_params=_params(2),
        name="block_attn_sample",
    )(table, *([pool_k] * pps), *([pool_v] * pps), q, k_new, v_new, mpast, mnew, aux)


def _round_up(x, m):
    return -(-x // m) * m


def _rows_last(a):
    n, rows = a.shape[:2]
    return jnp.transpose(a, (0, 2, 3, 1)).reshape(n * N_KV * HEAD_DIM, rows)


def _forward(x, nb, t, p0, past, prm):
    n = nb * t
    prompt = past is None
    tm = min(FFN_ROWS, n)
    if prompt:
        pos = jnp.arange(t, dtype=jnp.int32)
        n_tab = t // tm
    else:
        pos = p0 + jnp.arange(tm, dtype=jnp.int32) % t
        n_tab = 1
    tables = _rope_tables(pos)

    x = _ffn(x, prm["norm_ffn_a"][0], *prm["ffn_a"][0])
    cdim = prm["l0_conv_w"].shape[1]
    hq, hk = N_HEADS * HEAD_DIM, N_KV * HEAD_DIM
    gb, cu, q, k, v = _proj(functools.partial(_proj_ab_body, cdim), x, prm["norm_mix"][0], prm["l0_w_in"],
                            tables, n_tab, (cdim, cdim, hq, hk, hk))
    if prompt:
        tq = min(ATTN_ROWS, t)
        y_attn, y_conv = _window_attn(q, k, v, None, None, nb, t // tq, tq, SWA_WINDOW,
                                      conv=(gb, cu, None, prm["l0_conv_w"]), sinks=prm["l0_sinks"])
    else:
        conv_buf = jnp.pad(past["conv"], ((0, 0), (SUBLANES - (CONV_WIDTH - 1), 0), (0, 0)))
        y_attn, y_conv = _window_attn(q, k, v, _rows_last(past["swa_k"]), _rows_last(past["swa_v"]),
                                      nb, 1, t, SWA_WINDOW,
                                      conv=(gb, cu, conv_buf.reshape(nb * SUBLANES, cdim), prm["l0_conv_w"]),
                                      sinks=prm["l0_sinks"])
    x = _ffn(x, prm["norm_ffn_b"][0], *prm["ffn_b"][0], mix=(y_conv, y_attn, prm["l0_w_out"]))
    states0 = (cu, k, v)

    x = _ffn(x, prm["norm_ffn_a"][1], *prm["ffn_a"][1])
    qm, km, vm, qn, kc, vc, ks, vs, kw, vw, gates = _proj(
        _proj_cd_body, x, prm["norm_mix"][1], prm["l1_w_in"], tables, n_tab,
        (hq, hk, hk, hq, hk, hk, hk, hk, hk, hk, LANES))
    if prompt:
        tq = min(ATTN_ROWS, t)
        nt = t // tq
        n_blk = t // MOBA_BLOCK
        nbp = _round_up(n_blk + 1, LANES)
        means = _means_prompt(km, nb, t, nbp)
        moba_mask = _moba_select(qm, means, nb, nt, tq, 0, nbp)
        o_moba = _block_attn_prompt(qm, km, vm, moba_mask, nb, nt, tq, MOBA_BLOCK)
        ck = _compress_prompt(kc, nb, t, prm["cmp_k"])
        cv = _compress_prompt(vc, nb, t, prm["cmp_v"])
        n_ck = t // NSA_CMP_STRIDE
        nsp = _round_up(-(-t // NSA_SEL_BLOCK), LANES)
        o_cmp, sel_mask = _cmp_select(qn, ck, cv, nb, nt, tq, 0, n_ck, n_ck - 1, nsp)
        o_sel = _block_attn_prompt(qn, ks, vs, sel_mask, nb, nt, tq, NSA_SEL_BLOCK)
        (o_win,) = _window_attn(qn, kw, vw, None, None, nb, nt, tq, NSA_WINDOW)
    else:
        table = past["table"]
        n_past = p0
        n_blk = n_past // MOBA_BLOCK
        nbp = _round_up(n_blk + 1, LANES)
        km_t = jnp.transpose(km.reshape(nb, t, hk), (0, 2, 1)).reshape(nb * hk, t)
        means = _means_sample(past["moba_k"], table, km_t, nb, t, nbp)
        moba_mask = _moba_select(qm, means, nb, 1, t, p0, nbp)
        o_moba = _block_attn_sample(qm, past["moba_k"], past["moba_v"], table, km, vm, moba_mask, nb, t,
                                    MOBA_BLOCK)
        ck = _compress_sample(past["cmp_k"], table, kc, nb, t, prm["cmp_k"])
        cv = _compress_sample(past["cmp_v"], table, vc, nb, t, prm["cmp_v"])
        n_ck = n_past // NSA_CMP_STRIDE
        nsp = _round_up(-(-(n_past + t) // NSA_SEL_BLOCK), LANES)
        o_cmp, sel_mask = _cmp_select(qn, ck, cv, nb, 1, t, p0, n_ck, n_ck, nsp)
        o_sel = _block_attn_sample(qn, past["sel_k"], past["sel_v"], table, ks, vs, sel_mask, nb, t,
                                   NSA_SEL_BLOCK)
        (o_win,) = _window_attn(qn, kw, vw, _rows_last(past["win_k"]), _rows_last(past["win_v"]),
                                nb, 1, t, NSA_WINDOW)
    x = _ffn(x, prm["norm_ffn_b"][1], *prm["ffn_b"][1], mix=(o_moba, o_cmp, o_sel, o_win, gates, prm["l1_w_out"]),
             final_g=prm["norm_final"])
    return x, states0 + (km, vm, kc, vc, ks, vs, kw, vw)


def _tail(buf, new, rows):
    full = new if buf is None else jnp.concatenate([buf.astype(new.dtype), new], axis=1)
    return full[:, full.shape[1] - rows:]


def kernel(x_prompt, x_sample, state_l0_conv, cache_l0_swa_k, cache_l0_swa_v, cache_l1_moba_k, cache_l1_moba_v, cache_l1_nsa_cmp_k, cache_l1_nsa_cmp_v, cache_l1_nsa_sel_k, cache_l1_nsa_sel_v, cache_l1_nsa_win_k, cache_l1_nsa_win_v, page_table, norm_ffn_a, ffn_a_gate, ffn_a_up, ffn_a_down, norm_mix, norm_ffn_b, ffn_b_gate, ffn_b_up, ffn_b_down, norm_final, l0_w_in, l0_conv_w, l0_sinks, l0_w_out, l1_w_in, l1_cmp_k_w1, l1_cmp_k_b1, l1_cmp_k_w2, l1_cmp_v_w1, l1_cmp_v_b1, l1_cmp_v_w2, l1_w_out):
    bp, tp, d = x_prompt.shape
    bs, ts, _ = x_sample.shape
    depth = norm_ffn_a.shape[0]
    assert depth == 2 and N_KV * HEAD_DIM == LANES
    n_pool, page = cache_l1_moba_k.shape[:2]
    p0 = page_table.shape[1] * page
    hk = N_KV * HEAD_DIM

    w1_pad = _round_up(l1_w_in.shape[1], LANES) - l1_w_in.shape[1]
    prm = {
        "norm_ffn_a": norm_ffn_a, "norm_mix": norm_mix, "norm_ffn_b": norm_ffn_b, "norm_final": norm_final,
        "ffn_a": [(ffn_a_gate[l].astype(bf16), ffn_a_up[l].astype(bf16), ffn_a_down[l].astype(bf16))
                  for l in range(depth)],
        "ffn_b": [(ffn_b_gate[l].astype(bf16), ffn_b_up[l].astype(bf16), ffn_b_down[l].astype(bf16))
                  for l in range(depth)],
        "l0_w_in": l0_w_in.astype(bf16), "l0_conv_w": l0_conv_w, "l0_sinks": l0_sinks,
        "l0_w_out": l0_w_out.astype(bf16),
        "l1_w_in": jnp.pad(l1_w_in, ((0, 0), (0, w1_pad))).astype(bf16), "l1_w_out": l1_w_out.astype(bf16),
        "cmp_k": _cmp_weights(l1_cmp_k_w1, l1_cmp_k_b1, l1_cmp_k_w2),
        "cmp_v": _cmp_weights(l1_cmp_v_w1, l1_cmp_v_b1, l1_cmp_v_w2),
    }
    pool = lambda a: _rows_last(a).reshape(n_pool, hk, page)
    past = {
        "conv": state_l0_conv, "swa_k": cache_l0_swa_k, "swa_v": cache_l0_swa_v,
        "moba_k": pool(cache_l1_moba_k), "moba_v": pool(cache_l1_moba_v),
        "cmp_k": pool(cache_l1_nsa_cmp_k), "cmp_v": pool(cache_l1_nsa_cmp_v),
        "sel_k": pool(cache_l1_nsa_sel_k), "sel_v": pool(cache_l1_nsa_sel_v),
        "win_k": cache_l1_nsa_win_k, "win_v": cache_l1_nsa_win_v,
        "table": page_table.reshape(-1),
    }
    y_p, st_p = _forward(x_prompt.reshape(bp * tp, d), bp, tp, 0, None, prm)
    y_s, st_s = _forward(x_sample.reshape(bs * ts, d), bs, ts, p0, past, prm)

    def states(st, nb, t, bufs):
        cu, k0, v0, km, vm, kc, vc, ks, vs, kw, vw = st
        kv = lambda a: a.reshape(nb, t, N_KV, HEAD_DIM)
        conv = _tail(bufs["conv"], cu.reshape(nb, t, -1), CONV_WIDTH - 1)
        swa_rows = min(SWA_WINDOW, t) if bufs["swa_k"] is None else bufs["swa_k"].shape[1]
        win_rows = min(NSA_WINDOW, t) if bufs["win_k"] is None else bufs["win_k"].shape[1]
        return (conv, _tail(bufs["swa_k"], kv(k0), swa_rows), _tail(bufs["swa_v"], kv(v0), swa_rows),
                kv(km), kv(vm), kv(kc), kv(vc), kv(ks), kv(vs),
                _tail(bufs["win_k"], kv(kw), win_rows), _tail(bufs["win_v"], kv(vw), win_rows))

    none = {"conv": None, "swa_k": None, "swa_v": None, "win_k": None, "win_v": None}
    sp = states(st_p, bp, tp, none)
    ss = states(st_s, bs, ts, past)
    conv_p, swa_k_p, swa_v_p, mk_p, mv_p, ck_p, cv_p, sk_p, sv_p, wk_p, wv_p = sp
    conv_s, swa_k_s, swa_v_s, mk_s, mv_s, ck_s, cv_s, sk_s, sv_s, wk_s, wv_s = ss
    return (y_p.reshape(bp, tp, d), y_s.reshape(bs, ts, d), conv_p, conv_s, swa_k_p, swa_v_p, swa_k_s, swa_v_s,
            mk_p, mv_p, mk_s, mv_s, ck_p, cv_p, ck_s, cv_s,
            sk_p, sv_p, sk_s, sv_s, wk_p, wv_p, wk_s, wv_s)
```

```python
import functools
import math

import jax
import jax.numpy as jnp
from jax import lax
from jax.experimental import pallas as pl
from jax.experimental.pallas import tpu as pltpu

f32 = jnp.float32
bf16 = jnp.bfloat16

HEAD_DIM = 64
N_HEADS = 8
N_KV = 2
ROPE_THETA = 10000.0
NORM_EPS = 1e-5
CONV_WIDTH = 3
SWA_WINDOW = 128
MOBA_BLOCK = 256
MOBA_TOPK = 3
NSA_CMP_LEN = 32
NSA_CMP_STRIDE = 16
NSA_SEL_BLOCK = 64
NSA_TOPN = 16
NSA_WINDOW = 512
NEG_INF = -1e30
FORCE_SCORE = 1e9

LANES = 128
SUBLANES = 8
VMEM_LIMIT_BYTES = 60 * 1024 * 1024

FFN_ROWS = 512
PAGES_PER_STEP = 64
ATTN_ROWS = 256
SOFTMAX_ROWS = 64

_ARB = "arbitrary"


def _params(n_axes):
    return pltpu.CompilerParams(dimension_semantics=(_ARB,) * n_axes, vmem_limit_bytes=VMEM_LIMIT_BYTES)


def _log2(n):
    assert n > 0 and n & (n - 1) == 0, n
    return n.bit_length() - 1


def _dot(a, b):
    return jnp.dot(a, b, preferred_element_type=f32)


def _dot_nt(a, b):
    return lax.dot_general(a, b, (((1,), (1,)), ((), ())), preferred_element_type=f32)


def _split3(x):
    hi = x.astype(bf16)
    r1 = x - hi.astype(f32)
    mid = r1.astype(bf16)
    lo = (r1 - mid.astype(f32)).astype(bf16)
    return hi, mid, lo


def _rms(x, g):
    return x * lax.rsqrt(jnp.mean(x * x, axis=-1, keepdims=True) + NORM_EPS) * g


def _gate_expand(gates, comp):
    tm = gates.shape[0]
    head = jnp.right_shift(lax.broadcasted_iota(jnp.int32, (tm, N_HEADS * HEAD_DIM), 1), _log2(HEAD_DIM))
    out = jnp.zeros((tm, N_HEADS * HEAD_DIM), f32)
    for h in range(N_HEADS):
        col = gates[:, 3 * h + comp:3 * h + comp + 1]
        out = jnp.where(head == h, col, out)
    return out


def _ffn_body(mode, final, nf, *refs):
    it = iter(refs)
    x_ref = next(it)
    if mode == 1:
        pa_ref, pb_ref, wo_ref = next(it), next(it), next(it)
    elif mode == 2:
        om_ref, oc_ref, os_ref, ow_ref, gt_ref, wo_ref = (next(it) for _ in range(6))
    g_ref, wg_ref, wu_ref, wd_ref = next(it), next(it), next(it), next(it)
    gf_ref = next(it) if final else None
    o_ref, h_scr, acc_scr = next(it), next(it), next(it)
    j = pl.program_id(1)

    @pl.when(j == 0)
    def _():
        x1 = x_ref[...]
        if mode == 1:
            half = pa_ref.shape[1]
            x1 = x1 + _dot(pa_ref[...].astype(bf16), wo_ref[0:half, :])
            x1 = x1 + _dot(pb_ref[...].astype(bf16), wo_ref[half:2 * half, :])
        elif mode == 2:
            half = om_ref.shape[1]
            gates = gt_ref[...]
            y_nsa = (_gate_expand(gates, 0) * oc_ref[...] + _gate_expand(gates, 1) * os_ref[...]
                     + _gate_expand(gates, 2) * ow_ref[...])
            x1 = x1 + _dot(om_ref[...].astype(bf16), wo_ref[0:half, :])
            x1 = x1 + _dot(y_nsa.astype(bf16), wo_ref[half:2 * half, :])
        acc_scr[...] = x1
        h_scr[...] = _rms(x1, g_ref[...]).astype(bf16)

    h = h_scr[...]
    gate = _dot(h, wg_ref[...])
    up = _dot(h, wu_ref[...])
    act = gate * (1.0 / (1.0 + jnp.exp(-gate))) * up
    acc_scr[...] += 0.5 * _dot(act.astype(bf16), wd_ref[...])

    @pl.when(j == nf - 1)
    def _():
        y = acc_scr[...]
        if final:
            y = _rms(y, gf_ref[...])
        o_ref[...] = y


def _ffn_chunk(d_ff):
    best = LANES
    for c in range(LANES, 1536 + 1, LANES):
        if d_ff % c == 0:
            best = c
    return best if d_ff % LANES == 0 else d_ff


def _ffn(x, g, wg, wu, wd, mix=None, final_g=None):
    n, d = x.shape
    d_ff = wg.shape[1]
    tm = min(FFN_ROWS, n)
    tf = _ffn_chunk(d_ff)
    nf = d_ff // tf
    row = lambda i, j: (i, 0)
    const = lambda i, j: (0, 0)
    args, specs = [x], [pl.BlockSpec((tm, d), row)]
    mode = 0
    if mix is not None:
        mode = 1 if len(mix) == 3 else 2
        for a in mix[:-1]:
            args.append(a)
            specs.append(pl.BlockSpec((tm, a.shape[1]), row))
        args.append(mix[-1])
        specs.append(pl.BlockSpec(mix[-1].shape, const))
    args += [g.reshape(1, d), wg, wu, wd]
    specs += [pl.BlockSpec((1, d), const), pl.BlockSpec((d, tf), lambda i, j: (0, j)),
              pl.BlockSpec((d, tf), lambda i, j: (0, j)), pl.BlockSpec((tf, d), lambda i, j: (j, 0))]
    if final_g is not None:
        args.append(final_g.reshape(1, d))
        specs.append(pl.BlockSpec((1, d), const))
    return pl.pallas_call(
        functools.partial(_ffn_body, mode, final_g is not None, nf),
        grid=(n // tm, nf),
        in_specs=specs,
        out_specs=pl.BlockSpec((tm, d), row),
        out_shape=jax.ShapeDtypeStruct((n, d), f32),
        scratch_shapes=[pltpu.VMEM((tm, d), bf16), pltpu.VMEM((tm, d), f32)],
        compiler_params=_params(2),
        name="ffn",
    )(*args)


def _rope_tables(pos):
    half = HEAD_DIM // 2
    inv_freq = ROPE_THETA ** (-jnp.arange(half, dtype=f32) * 2.0 / HEAD_DIM)
    ang = pos.astype(f32)[:, None] * inv_freq[None, :]
    cos, sin = jnp.cos(ang), jnp.sin(ang)
    zero = jnp.zeros_like(sin)
    reps = LANES // HEAD_DIM
    return (jnp.concatenate([cos, cos] * reps, axis=1), jnp.concatenate([-sin, zero] * reps, axis=1),
            jnp.concatenate([zero, sin] * reps, axis=1))


def _rope(x, cos, sa, sb):
    half = HEAD_DIM // 2
    outs = []
    for c in range(x.shape[1] // LANES):
        xc = x[:, c * LANES:(c + 1) * LANES]
        outs.append(xc * cos + pltpu.roll(xc, LANES - half, 1) * sa + pltpu.roll(xc, half, 1) * sb)
    return outs[0] if len(outs) == 1 else jnp.concatenate(outs, axis=1)


def _proj_ab_body(cdim, x_ref, g_ref, w_ref, cos_ref, sa_ref, sb_ref, gb_ref, cu_ref, q_ref, k_ref, v_ref):
    h = _rms(x_ref[...], g_ref[...]).astype(bf16)
    cos, sa, sb = cos_ref[...], sa_ref[...], sb_ref[...]
    seg = lambda off, width: _dot(h, w_ref[:, off:off + width])
    hq, hk = q_ref.shape[1], k_ref.shape[1]
    gb_ref[...] = seg(0, cdim)
    cu_ref[...] = seg(cdim, cdim) * seg(2 * cdim, cdim)
    q_ref[...] = _rope(seg(3 * cdim, hq), cos, sa, sb)
    k_ref[...] = _rope(seg(3 * cdim + hq, hk), cos, sa, sb)
    v_ref[...] = seg(3 * cdim + hq + hk, hk)


def _proj_cd_body(x_ref, g_ref, w_ref, cos_ref, sa_ref, sb_ref,
                  qm_ref, km_ref, vm_ref, qn_ref, kc_ref, vc_ref, ks_ref, vs_ref, kw_ref, vw_ref, gt_ref):
    h = _rms(x_ref[...], g_ref[...]).astype(bf16)
    cos, sa, sb = cos_ref[...], sa_ref[...], sb_ref[...]
    off = 0
    for ref, kind in ((qm_ref, "rope"), (km_ref, "rope"), (vm_ref, ""), (qn_ref, "rope"), (kc_ref, ""),
                      (vc_ref, ""), (ks_ref, "rope"), (vs_ref, ""), (kw_ref, "rope"), (vw_ref, ""),
                      (gt_ref, "sigmoid")):
        width = ref.shape[1]
        z = _dot(h, w_ref[:, off:off + width])
        if kind == "rope":
            z = _rope(z, cos, sa, sb)
        elif kind == "sigmoid":
            z = 1.0 / (1.0 + jnp.exp(-z))
        ref[...] = z
        off += width


def _proj(body, x, g, w, tables, n_tab_tiles, widths):
    n, d = x.shape
    tm = min(FFN_ROWS, n)
    row = lambda i: (i, 0)
    const = lambda i: (0, 0)
    tab = lambda i: (i % n_tab_tiles, 0)
    return pl.pallas_call(
        body,
        grid=(n // tm,),
        in_specs=[pl.BlockSpec((tm, d), row), pl.BlockSpec((1, d), const), pl.BlockSpec(w.shape, const)]
        + [pl.BlockSpec((tm, LANES), tab)] * 3,
        out_specs=[pl.BlockSpec((tm, wd), row) for wd in widths],
        out_shape=[jax.ShapeDtypeStruct((n, wd), f32) for wd in widths],
        compiler_params=_params(1),
        name="proj",
    )(x, g.reshape(1, d), w, *tables)


def _build_qbd(q, scale):
    tq = q.shape[0]
    lo = lax.broadcasted_iota(jnp.int32, (tq, LANES), 1) < HEAD_DIM
    per = N_HEADS // N_KV
    pieces = []
    for h in range(N_HEADS):
        sl = q[:, LANES * (h // 2):LANES * (h // 2) + LANES]
        src_hi, dst_hi = (h % 2 == 1), (h // per == 1)
        if src_hi != dst_hi:
            sl = pltpu.roll(sl, HEAD_DIM, 1)
        pieces.append(jnp.where(lo != dst_hi, sl, 0.0))
    return (jnp.concatenate(pieces, axis=0) * scale).astype(bf16)


def _unstack(o, tq):
    lo = lax.broadcasted_iota(jnp.int32, (tq, LANES), 1) < HEAD_DIM
    per = N_HEADS // N_KV
    outs = []
    for c in range(N_HEADS // 2):
        a = o[(2 * c) * tq:(2 * c + 1) * tq]
        b = o[(2 * c + 1) * tq:(2 * c + 2) * tq]
        if (2 * c) // per == 1:
            a = pltpu.roll(a, HEAD_DIM, 1)
        if (2 * c + 1) // per == 0:
            b = pltpu.roll(b, HEAD_DIM, 1)
        outs.append(jnp.where(lo, a, b))
    return jnp.concatenate(outs, axis=1)


def _row_token(rows, tq, width):
    _log2(tq)
    return lax.broadcasted_iota(jnp.int32, (rows, width), 0) & (tq - 1)


def _attn_update(qbd, k, v, allowed, m, l, acc, transposed=False):
    k, v = k.astype(bf16), v.astype(bf16)
    s = jnp.where(allowed, _dot(qbd, k) if transposed else _dot_nt(qbd, k), NEG_INF)
    m_new = jnp.maximum(m, jnp.max(s, axis=-1, keepdims=True))
    alpha = jnp.exp(m - m_new)
    p = jnp.where(allowed, jnp.exp(s - m_new), 0.0)
    l_new = alpha * l + jnp.sum(p, axis=-1, keepdims=True)
    p = p.astype(bf16)
    acc_new = alpha * acc + (_dot_nt(p, v) if transposed else _dot(p, v))
    return m_new, l_new, acc_new


def _pad_rows(x, rows):
    if x.shape[0] >= rows:
        return x
    return jnp.concatenate([x, jnp.zeros((rows - x.shape[0], x.shape[1]), x.dtype)], axis=0)


def _expand_mask(m_rows, key_pos, blk_shift):
    nbp, tk = m_rows.shape[1], key_pos.shape[1]
    blk = jnp.right_shift(key_pos, blk_shift)
    e = jnp.where(lax.broadcasted_iota(jnp.int32, (nbp, tk), 0) == blk, 1.0, 0.0).astype(bf16)
    return _dot(m_rows, e) > 0.5


def _win_body(tq, window, hb, nh, hist_always, with_conv, with_sinks, *refs):
    it = iter(refs)
    q_ref, kc_ref, vc_ref = (next(it) for _ in range(3))
    kh_refs = [next(it) for _ in range(nh)]
    vh_refs = [next(it) for _ in range(nh)]
    if with_conv:
        gb_ref, cu_ref, cup_ref, cw_ref = (next(it) for _ in range(4))
    sink_ref = next(it) if with_sinks else None
    o_ref = next(it)
    oc_ref = next(it) if with_conv else None
    i = pl.program_id(1)
    rows = N_HEADS * tq

    qbd = _build_qbd(q_ref[...], HEAD_DIM ** -0.5)
    m = jnp.full((rows, 1), NEG_INF, f32)
    l = jnp.zeros((rows, 1), f32)
    acc = jnp.zeros((rows, LANES), f32)
    t_h = _row_token(rows, tq, hb)
    c_h = lax.broadcasted_iota(jnp.int32, (rows, hb), 1)
    for j in range(nh):
        ok = (c_h - (nh - j) * hb >= t_h - window)
        if not hist_always:
            ok = ok & (i * (tq // hb) >= nh - j)
        m, l, acc = _attn_update(qbd, kh_refs[j][...], vh_refs[j][...], ok, m, l, acc, transposed=hist_always)
    hist_ok = True if hist_always else (i > 0)
    tk = max(tq, LANES)
    t_c = _row_token(rows, tq, tk)
    c_c = lax.broadcasted_iota(jnp.int32, (rows, tk), 1)
    allowed = (c_c <= t_c) & (c_c >= t_c - window) & (c_c < tq)
    m, l, acc = _attn_update(qbd, _pad_rows(kc_ref[...], tk), _pad_rows(vc_ref[...], tk), allowed, m, l, acc)
    if with_sinks:
        head = jnp.right_shift(lax.broadcasted_iota(jnp.int32, (rows, 1), 0), _log2(tq))
        sink = jnp.zeros((rows, 1), f32)
        for h in range(N_HEADS):
            sink = jnp.where(head == h, sink_ref[h], sink)
        m_f = jnp.maximum(m, sink)
        scale = jnp.exp(m - m_f)
        out = acc * scale / (l * scale + jnp.exp(sink - m_f))
    else:
        out = acc / l
    o_ref[...] = _unstack(out, tq)

    if with_conv:
        cu = cu_ref[...]
        prev = jnp.where(hist_ok, cup_ref[...], 0.0)
        r = lax.broadcasted_iota(jnp.int32, cu.shape, 0)
        p1 = prev[SUBLANES - 1:SUBLANES, :]
        p2 = prev[SUBLANES - 2:SUBLANES - 1, :]
        cu_m1 = jnp.where(r == 0, p1, pltpu.roll(cu, 1, 0))
        cu_m2 = jnp.where(r == 0, p2, jnp.where(r == 1, p1, pltpu.roll(cu, 2, 0)))
        cw = cw_ref[...]
        oc_ref[...] = gb_ref[...] * (cu_m2 * cw[0:1, :] + cu_m1 * cw[1:2, :] + cu * cw[2:3, :])


def _window_attn(q, k, v, k_hist, v_hist, nb, nt, tq, window, conv=None, sinks=None):
    n = q.shape[0]
    hq = q.shape[1]
    per_seq = nt * tq
    cur = lambda b, i: (b * nt + i, 0)
    if k_hist is None:
        hist_always = False
        hb = min(window, tq)
        nh = window // hb
        assert tq % hb == 0 and window % hb == 0

        def hist_map(j):
            return lambda b, i: (b * (per_seq // hb) + jnp.maximum(i * (tq // hb) - (nh - j), 0), 0)

        k_hist, v_hist = k, v
    else:
        hist_always = True
        hb, nh = window, 1
        hist_map = lambda j: (lambda b, i: (b, 0))
    hist_block = (LANES, hb) if hist_always else (hb, LANES)
    args = [q, k, v] + [k_hist] * nh + [v_hist] * nh
    specs = ([pl.BlockSpec((tq, hq), cur), pl.BlockSpec((tq, LANES), cur), pl.BlockSpec((tq, LANES), cur)]
             + [pl.BlockSpec(hist_block, hist_map(j)) for j in range(nh)] * 2)
    out_shape = [jax.ShapeDtypeStruct((n, hq), f32)]
    out_specs = [pl.BlockSpec((tq, hq), cur)]
    if conv is not None:
        gb, cu, cu_prev, cw = conv
        cdim = gb.shape[1]
        if cu_prev is None:
            prev = lambda b, i: (jnp.maximum((b * per_seq + i * tq) // SUBLANES - 1, 0), 0)
            cu_prev = cu
        else:
            prev = lambda b, i: (b, 0)
        args += [gb, cu, cu_prev, cw]
        specs += [pl.BlockSpec((tq, cdim), cur), pl.BlockSpec((tq, cdim), cur),
                  pl.BlockSpec((SUBLANES, cdim), prev), pl.BlockSpec(cw.shape, lambda b, i: (0, 0))]
        out_shape.append(jax.ShapeDtypeStruct((n, cdim), f32))
        out_specs.append(pl.BlockSpec((tq, cdim), cur))
    if sinks is not None:
        args.append(sinks)
        specs.append(pl.BlockSpec(memory_space=pltpu.SMEM))
    return pl.pallas_call(
        functools.partial(_win_body, tq, window, hb, nh, hist_always, conv is not None, sinks is not None),
        grid=(nb, nt),
        in_specs=specs,
        out_specs=out_specs,
        out_shape=out_shape,
        compiler_params=_params(2),
        name="window_attn",
    )(*args)


def _cmp_weights(w1, b1, w2):
    n_half = NSA_CMP_LEN // NSA_CMP_STRIDE
    hid = w1.shape[2]
    w1h = w1.reshape(n_half, NSA_CMP_STRIDE, HEAD_DIM, hid)
    big = jnp.zeros((NSA_CMP_STRIDE, N_KV * HEAD_DIM, n_half * N_KV * hid), f32)
    for hh in range(n_half):
        for g in range(N_KV):
            big = big.at[:, g * HEAD_DIM:(g + 1) * HEAD_DIM,
                         hh * N_KV * hid + g * hid:hh * N_KV * hid + (g + 1) * hid].set(w1h[hh])
    w2bd = jnp.zeros((N_KV * hid, N_KV * HEAD_DIM), f32)
    for g in range(N_KV):
        w2bd = w2bd.at[g * hid:(g + 1) * hid, g * HEAD_DIM:(g + 1) * HEAD_DIM].set(w2)
    return big.astype(bf16), jnp.tile(b1, N_KV).reshape(1, N_KV * hid), w2bd.astype(bf16)


def _cmp_finish(hp, n_tok, b1, w2bd):
    rows = hp.shape[0]
    hp1 = pltpu.roll(hp[:, LANES:], rows - 1, 0)
    hid = jax.nn.gelu(hp[:, :LANES] + hp1 + b1)
    return _dot(hid.astype(bf16), w2bd)[:n_tok]


def _cmp_prompt_body(n_chunk, x_ref, w1_ref, b1_ref, w2_ref, o_ref):
    hp = jnp.zeros((n_chunk, 2 * LANES), f32)
    for s in range(NSA_CMP_STRIDE):
        hp = hp + _dot(x_ref[pl.ds(s, n_chunk, stride=NSA_CMP_STRIDE), :].astype(bf16), w1_ref[s])
    r = lax.broadcasted_iota(jnp.int32, (n_chunk, LANES), 0)
    hp1 = jnp.where(r == n_chunk - 1, 0.0, pltpu.roll(hp[:, LANES:], n_chunk - 1, 0))
    hid = jax.nn.gelu(hp[:, :LANES] + hp1 + b1_ref[...])
    o_ref[...] = _dot(hid.astype(bf16), w2_ref[...])


def _compress_prompt(x, nb, t, wts):
    w1, b1, w2 = wts
    n_chunk = t // NSA_CMP_STRIDE
    c3 = lambda b: (0, 0, 0)
    c2 = lambda b: (0, 0)
    return pl.pallas_call(
        functools.partial(_cmp_prompt_body, n_chunk),
        grid=(nb,),
        in_specs=[pl.BlockSpec((t, LANES), lambda b: (b, 0)), pl.BlockSpec(w1.shape, c3),
                  pl.BlockSpec(b1.shape, c2), pl.BlockSpec(w2.shape, c2)],
        out_specs=pl.BlockSpec((n_chunk, LANES), lambda b: (b, 0)),
        out_shape=jax.ShapeDtypeStruct((nb * n_chunk, LANES), f32),
        compiler_params=_params(1),
        name="compress_prompt",
    )(x, w1, b1, w2)


def _paged_scratch(n_pools, pps, page):
    return ([pltpu.VMEM((2, pps, LANES, page), f32) for _ in range(n_pools)]
            + [pltpu.SemaphoreType.DMA((n_pools, 2))])


def _page_copy(pool, buf, sem, k, slot, src_page, i):
    return pltpu.make_async_copy(pool.at[src_page], buf.at[slot, i], sem.at[k, slot])


def _paged_step(pools, bufs, sem, tab_ref, pps):
    nc = pl.num_programs(1)
    s = pl.program_id(0) * nc + pl.program_id(1)
    total = pl.num_programs(0) * nc
    slot = lax.rem(s, 2)

    def fetch(step, sl):
        def start(i, carry):
            src_page = tab_ref[step * pps + i]
            for k, (pool, buf) in enumerate(zip(pools, bufs)):
                _page_copy(pool, buf, sem, k, sl, src_page, i).start()
            return carry
        lax.fori_loop(0, pps, start, 0)

    @pl.when(s == 0)
    def _():
        fetch(0, 0)

    @pl.when(s + 1 < total)
    def _():
        fetch(s + 1, 1 - slot)

    for k, (pool, buf) in enumerate(zip(pools, bufs)):
        pltpu.make_async_copy(pool.at[pl.ds(0, pps)], buf.at[slot], sem.at[k, slot]).wait()
    return slot


def _cmp_sample_body(pps, nc, page, n_past_chunk, tab_ref, pool_ref, xn_ref, w1_ref, b1_ref, w2_ref, o_ref,
                     hp_scr, buf, sem):
    c = pl.program_id(1)
    slot = _paged_step([pool_ref], [buf], sem, tab_ref, pps)
    pages = [buf.at[slot, i] for i in range(pps)]
    cpp = page // NSA_CMP_STRIDE
    out_r = lax.broadcasted_iota(jnp.int32, (page, page), 0)
    src_r = lax.broadcasted_iota(jnp.int32, (page, page), 1)
    src_of = (out_r & (cpp - 1)) * NSA_CMP_STRIDE + jnp.right_shift(out_r, _log2(cpp))
    perm = jnp.where(src_r == src_of, 1.0, 0.0).astype(bf16)
    regrouped = [_dot_nt(perm, p[...].astype(bf16)) for p in pages]
    flat = jnp.concatenate([jnp.concatenate([x[s * cpp:(s + 1) * cpp] for x in regrouped], axis=0)
                            for s in range(NSA_CMP_STRIDE)], axis=1)
    hp = _dot(flat.astype(bf16), w1_ref[...].reshape(NSA_CMP_STRIDE * LANES, 2 * LANES))
    hp_scr[pl.ds(pl.multiple_of(c * (pps * cpp), SUBLANES), pps * cpp), :] = hp

    @pl.when(c == nc - 1)
    def _():
        xn = xn_ref[...].astype(bf16)
        r = lax.broadcasted_iota(jnp.int32, (NSA_CMP_STRIDE, 2 * LANES), 0)
        new = jnp.zeros((NSA_CMP_STRIDE, 2 * LANES), f32)
        for s in range(NSA_CMP_STRIDE):
            new = new + jnp.where(r == s, _dot(xn, w1_ref[s]), 0.0)
        new = jnp.sum(new, axis=0, keepdims=True)
        r8 = lax.broadcasted_iota(jnp.int32, (SUBLANES, 2 * LANES), 0)
        hp_scr[n_past_chunk:n_past_chunk + SUBLANES, :] = jnp.where(r8 == 0, new, 0.0)
        o_ref[...] = _cmp_finish(hp_scr[...], n_past_chunk, b1_ref[...], w2_ref[...])


def _compress_sample(pool, table, x_new, nb, t_new, wts):
    w1, b1, w2 = wts
    page = pool.shape[2]
    n_pages = table.shape[0] // nb
    pps = min(PAGES_PER_STEP, n_pages)
    nc = n_pages // pps
    n_past_chunk = n_pages * page // NSA_CMP_STRIDE
    assert t_new <= NSA_CMP_STRIDE
    xn = jnp.pad(x_new.reshape(nb, t_new, LANES), ((0, 0), (0, NSA_CMP_STRIDE - t_new), (0, 0)))
    xn = xn.reshape(nb * NSA_CMP_STRIDE, LANES)
    c3 = lambda b, c, tab: (0, 0, 0)
    c2 = lambda b, c, tab: (0, 0)
    grid_spec = pltpu.PrefetchScalarGridSpec(
        num_scalar_prefetch=1,
        grid=(nb, nc),
        in_specs=[pl.BlockSpec(memory_space=pl.ANY),
                  pl.BlockSpec((NSA_CMP_STRIDE, LANES), lambda b, c, tab: (b, 0)), pl.BlockSpec(w1.shape, c3),
                  pl.BlockSpec(b1.shape, c2), pl.BlockSpec(w2.shape, c2)],
        out_specs=pl.BlockSpec((n_past_chunk, LANES), lambda b, c, tab: (b, 0)),
        scratch_shapes=[pltpu.VMEM((n_past_chunk + SUBLANES, 2 * LANES), f32)] + _paged_scratch(1, pps, page),
    )
    return pl.pallas_call(
        functools.partial(_cmp_sample_body, pps, nc, page, n_past_chunk),
        grid_spec=grid_spec,
        out_shape=jax.ShapeDtypeStruct((nb * n_past_chunk, LANES), f32),
        compiler_params=_params(2),
        name="compress_sample",
    )(table, pool, xn, w1, b1, w2)


def _means_prompt_body(n_blk, nbp, k_ref, o_ref):
    t = k_ref.shape[0]
    sums = jnp.sum(k_ref[...].reshape(n_blk, t // n_blk, LANES), axis=1)
    o_ref[...] = _pad_rows(sums * (1.0 / MOBA_BLOCK), nbp).T


def _means_prompt(k, nb, t, nbp):
    n_blk = t // MOBA_BLOCK
    return pl.pallas_call(
        functools.partial(_means_prompt_body, n_blk, nbp),
        grid=(nb,),
        in_specs=[pl.BlockSpec((t, LANES), lambda b: (b, 0))],
        out_specs=pl.BlockSpec((LANES, nbp), lambda b: (b, 0)),
        out_shape=jax.ShapeDtypeStruct((nb * LANES, nbp), f32),
        compiler_params=_params(1),
        name="moba_means_prompt",
    )(k)


def _means_sample_body(pps, nc, n_past_blk, nbp, tab_ref, pool_ref, kn_ref, o_ref, buf, sem):
    c = pl.program_id(1)
    slot = _paged_step([pool_ref], [buf], sem, tab_ref, pps)
    pages = [buf.at[slot, i] for i in range(pps)]
    per = pps // 2

    lane = lax.broadcasted_iota(jnp.int32, (LANES, nbp), 1)
    @pl.when(c == 0)
    def _():
        o_ref[...] = jnp.zeros((LANES, nbp), f32)

    acc = o_ref[...]
    for i in range(per):
        col = jnp.sum(pages[2 * i][...] + pages[2 * i + 1][...], axis=1, keepdims=True) * (1.0 / MOBA_BLOCK)
        acc = jnp.where(lane == c * per + i, col, acc)
    new = jnp.sum(kn_ref[...], axis=1, keepdims=True) * (1.0 / MOBA_BLOCK)
    acc = jnp.where((lane == n_past_blk) & (c == nc - 1), new, acc)
    o_ref[...] = acc


def _means_sample(pool, table, k_new_t, nb, t_new, nbp):
    page = pool.shape[2]
    n_pages = table.shape[0] // nb
    pps = min(PAGES_PER_STEP, n_pages)
    nc = n_pages // pps
    assert MOBA_BLOCK == 2 * page and t_new <= MOBA_BLOCK and pps % 2 == 0
    n_past_blk = n_pages * page // MOBA_BLOCK
    grid_spec = pltpu.PrefetchScalarGridSpec(
        num_scalar_prefetch=1,
        grid=(nb, nc),
        in_specs=[pl.BlockSpec(memory_space=pl.ANY), pl.BlockSpec((LANES, t_new), lambda b, c, tab: (b, 0))],
        out_specs=pl.BlockSpec((LANES, nbp), lambda b, c, tab: (b, 0)),
        scratch_shapes=_paged_scratch(1, pps, page),
    )
    return pl.pallas_call(
        functools.partial(_means_sample_body, pps, nc, n_past_blk, nbp),
        grid_spec=grid_spec,
        out_shape=jax.ShapeDtypeStruct((nb * LANES, nbp), f32),
        compiler_params=_params(2),
        name="moba_means_sample",
    )(table, pool, k_new_t)


def _topk_cols(score, k):
    n = score.shape[0]
    ridx = lax.broadcasted_iota(jnp.int32, score.shape, 0)
    sel = jnp.zeros(score.shape, f32)
    work = score
    for _ in range(k):
        mx = jnp.max(work, axis=0, keepdims=True)
        cand = (work == mx) & (mx > -jnp.inf)
        first = jnp.min(jnp.where(cand, ridx, n), axis=0, keepdims=True)
        pick = ridx == first
        sel = jnp.where(pick, 1.0, sel)
        work = jnp.where(pick, -jnp.inf, work)
    return sel


def _moba_sel_body(tq, p0, nbp, q_ref, mean_ref, o_ref):
    i = pl.program_id(1)
    rows = N_HEADS * tq
    cols = max(rows, LANES)
    qbd = _pad_rows(_build_qbd(q_ref[...], 1.0).astype(f32), cols)
    q_hi = qbd.astype(bf16)
    q_lo = (qbd - q_hi.astype(f32)).astype(bf16)
    mean = mean_ref[...]
    m_hi = mean.astype(bf16)
    m_lo = (mean - m_hi.astype(f32)).astype(bf16)
    s = (_dot(q_hi, m_hi) + _dot(q_lo, m_hi) + _dot(q_hi, m_lo)).T
    n_idx = lax.broadcasted_iota(jnp.int32, (nbp, cols), 0)
    t = lax.broadcasted_iota(jnp.int32, (nbp, cols), 1) & (tq - 1)
    cur = jnp.right_shift(p0 + i * tq + t, _log2(MOBA_BLOCK))
    sel = _topk_cols(jnp.where(n_idx < cur, s, -jnp.inf), MOBA_TOPK)
    sel = jnp.where(n_idx == cur, 1.0, sel)
    sel_t = sel.T
    for h in range(N_HEADS):
        o_ref[h] = sel_t[h * tq:(h + 1) * tq]


def _moba_select(q, means, nb, nt, tq, p0, nbp):
    n, hq = q.shape
    return pl.pallas_call(
        functools.partial(_moba_sel_body, tq, p0, nbp),
        grid=(nb, nt),
        in_specs=[pl.BlockSpec((tq, hq), lambda b, i: (b * nt + i, 0)),
                  pl.BlockSpec((LANES, nbp), lambda b, i: (b, 0))],
        out_specs=pl.BlockSpec((N_HEADS, tq, nbp), lambda b, i: (0, b * nt + i, 0)),
        out_shape=jax.ShapeDtypeStruct((N_HEADS, n, nbp), f32),
        compiler_params=_params(2),
        name="moba_select",
    )(q, means)


def _cmp_sel_body(tq, p0, n_ck, n_cmp, nsp, q_ref, ck_ref, cv_ref, o_ref, m_ref):
    i = pl.program_id(1)
    rows = N_HEADS * tq
    per = N_HEADS // N_KV
    qbd = _build_qbd(q_ref[...], HEAD_DIM ** -0.5)
    n_idx = lax.broadcasted_iota(jnp.int32, (rows, n_ck), 1)
    pos = p0 + i * tq + _row_token(rows, tq, n_ck)
    valid = (n_idx * NSA_CMP_STRIDE + (NSA_CMP_LEN - 1) <= pos) & (n_idx < n_cmp)
    s = jnp.where(valid, _dot_nt(qbd, ck_ref[...].astype(bf16)), NEG_INF)
    mx = jnp.max(s, axis=-1, keepdims=True)
    e = jnp.where(valid, jnp.exp(s - mx), 0.0)
    den = jnp.sum(e, axis=-1, keepdims=True)
    p = e * jnp.where(den > 0.0, 1.0 / den, 0.0)
    o_ref[...] = _unstack(_dot(p.astype(bf16), cv_ref[...].astype(bf16)), tq)

    cols = max(tq, LANES)
    j_idx = lax.broadcasted_iota(jnp.int32, (nsp, n_ck), 0)
    n_of = lax.broadcasted_iota(jnp.int32, (nsp, n_ck), 1)
    ratio = NSA_SEL_BLOCK // NSA_CMP_STRIDE
    m_span = NSA_CMP_LEN // NSA_CMP_STRIDE
    lo = ratio * j_idx - (m_span - 1)
    a_t = jnp.where((n_of >= lo) & (n_of <= lo + ratio + m_span - 2), 1.0, 0.0).astype(bf16)
    j_col = lax.broadcasted_iota(jnp.int32, (nsp, N_KV * cols), 0)
    t_col = lax.broadcasted_iota(jnp.int32, (nsp, N_KV * cols), 1) & (cols - 1)
    j_cur = jnp.right_shift(p0 + i * tq + t_col, _log2(NSA_SEL_BLOCK))
    forced = (j_col == 0) | (j_col == j_cur) | (j_col == j_cur - 1)
    p_slc = []
    for g in range(N_KV):
        imp = p[(g * per) * tq:(g * per + 1) * tq]
        for r in range(1, per):
            imp = imp + p[(g * per + r) * tq:(g * per + r + 1) * tq]
        imp = _pad_rows(imp, cols)
        hi, mid, lo3 = _split3(imp)
        p_slc.append(_dot_nt(a_t, hi) + _dot_nt(a_t, mid) + _dot_nt(a_t, lo3))
    score = jnp.where(forced, FORCE_SCORE, jnp.where(j_col <= j_cur, jnp.concatenate(p_slc, axis=1), -jnp.inf))
    sel = _topk_cols(score, NSA_TOPN)
    for g in range(N_KV):
        sel_t = sel[:, g * cols:(g + 1) * cols].T[:tq]
        for r in range(per):
            m_ref[g * per + r] = sel_t


def _cmp_select(q, ck, cv, nb, nt, tq, p0, n_ck, n_cmp, nsp):
    n, hq = q.shape
    cur = lambda b, i: (b * nt + i, 0)
    return pl.pallas_call(
        functools.partial(_cmp_sel_body, tq, p0, n_ck, n_cmp, nsp),
        grid=(nb, nt),
        in_specs=[pl.BlockSpec((tq, hq), cur), pl.BlockSpec((n_ck, LANES), lambda b, i: (b, 0)),
                  pl.BlockSpec((n_ck, LANES), lambda b, i: (b, 0))],
        out_specs=[pl.BlockSpec((tq, hq), cur),
                   pl.BlockSpec((N_HEADS, tq, nsp), lambda b, i: (0, b * nt + i, 0))],
        out_shape=[jax.ShapeDtypeStruct((n, hq), f32), jax.ShapeDtypeStruct((N_HEADS, n, nsp), f32)],
        compiler_params=_params(2),
        name="cmp_select",
    )(q, ck, cv)


def _blk_prompt_body(tq, blk_shift, q_ref, k_ref, v_ref, mask_ref, o_ref,
                     kaug_scr, vaug_scr, qaug_scr, s_scr, p_scr, a_scr, m_scr, acc_scr):
    i = pl.program_id(1)
    rows = N_HEADS * tq
    nbp = mask_ref.shape[2]
    t_len = k_ref.shape[0]

    @pl.when(i == 0)
    def _():
        key = lax.broadcasted_iota(jnp.int32, (nbp, t_len), 1)
        blk = lax.broadcasted_iota(jnp.int32, (nbp, t_len), 0)
        kaug_scr[:LANES, :] = k_ref[...].T.astype(bf16)
        kaug_scr[LANES:, :] = jnp.where(blk == jnp.right_shift(key, blk_shift), NEG_INF, 0.0).astype(bf16)
        ones = jnp.where(lax.broadcasted_iota(jnp.int32, (t_len, LANES), 1) == 0, 1.0, 0.0)
        vaug_scr[:, :LANES] = v_ref[...].astype(bf16)
        vaug_scr[:, LANES:] = ones.astype(bf16)

    unselected = 1.0 - mask_ref[...].reshape(rows, nbp)
    qbd = _build_qbd(q_ref[...], HEAD_DIM ** -0.5 * math.log2(math.e))
    qaug_scr[...] = jnp.concatenate([qbd, unselected.astype(bf16)], axis=1)
    m_scr[...] = jnp.full((rows, LANES), NEG_INF, f32)
    acc_scr[...] = jnp.zeros((rows, 2 * LANES), f32)

    def tile(j, causal):
        st = pl.multiple_of(j * tq, tq)
        s_scr[...] = _dot(qaug_scr[...], kaug_scr[:, pl.ds(st, tq)])
        for r in range(rows // SOFTMAX_ROWS):
            sl = slice(r * SOFTMAX_ROWS, (r + 1) * SOFTMAX_ROWS)
            s = s_scr[sl, :]
            if causal:
                t_row = (r * SOFTMAX_ROWS + lax.broadcasted_iota(jnp.int32, (SOFTMAX_ROWS, tq), 0)) & (tq - 1)
                s = jnp.where(lax.broadcasted_iota(jnp.int32, (SOFTMAX_ROWS, tq), 1) <= t_row, s, NEG_INF)
            m_old = m_scr[sl, :]
            m_new = jnp.maximum(m_old, jnp.max(s, axis=-1, keepdims=True))
            p_scr[sl, :] = jnp.exp2(s - jnp.concatenate([m_new] * (tq // LANES), axis=1)).astype(bf16)
            a_scr[sl, :] = jnp.exp2(m_old - m_new)
            m_scr[sl, :] = m_new
        a = a_scr[...]
        acc_scr[...] = jnp.concatenate([a, a], axis=1) * acc_scr[...] + _dot(p_scr[...], vaug_scr[pl.ds(st, tq), :])

    tile(i, True)
    lax.fori_loop(0, i, lambda j, carry: (tile(j, False), carry)[1], 0)
    acc = acc_scr[...]
    o_ref[...] = _unstack(acc[:, :LANES] / acc[:, LANES:LANES + 1], tq)


def _block_attn_prompt(q, k, v, mask, nb, nt, tq, blk):
    n, hq = q.shape
    t = nt * tq
    nbp = mask.shape[2]
    assert nbp == LANES
    rows = N_HEADS * tq
    cur = lambda b, i: (b * nt + i, 0)
    seq = lambda b, i: (b, 0)
    return pl.pallas_call(
        functools.partial(_blk_prompt_body, tq, _log2(blk)),
        grid=(nb, nt),
        in_specs=[pl.BlockSpec((tq, hq), cur), pl.BlockSpec((t, LANES), seq), pl.BlockSpec((t, LANES), seq),
                  pl.BlockSpec((N_HEADS, tq, nbp), lambda b, i: (0, b * nt + i, 0))],
        out_specs=pl.BlockSpec((tq, hq), cur),
        out_shape=jax.ShapeDtypeStruct((n, hq), f32),
        scratch_shapes=[pltpu.VMEM((2 * LANES, t), bf16), pltpu.VMEM((t, 2 * LANES), bf16),
                        pltpu.VMEM((rows, 2 * LANES), bf16), pltpu.VMEM((rows, tq), f32),
                        pltpu.VMEM((rows, tq), bf16), pltpu.VMEM((rows, LANES), f32),
                        pltpu.VMEM((rows, LANES), f32), pltpu.VMEM((rows, 2 * LANES), f32)],
        compiler_params=_params(2),
        name="block_attn_prompt",
    )(q, k, v, mask)


def _blk_sample_body(pps, nc, tq, blk_shift, tab_ref, kpool_ref, vpool_ref, q_ref, kn_ref, vn_ref, mpast_ref,
                     mnew_ref, aux_ref, o_ref, qbd_scr, m_scr, acc_scr, kbuf, vbuf, sem):
    c = pl.program_id(1)
    rows = N_HEADS * tq
    slot = _paged_step([kpool_ref, vpool_ref], [kbuf, vbuf], sem, tab_ref, pps)
    kp = [kbuf.at[slot, i] for i in range(pps)]
    vp = [vbuf.at[slot, i] for i in range(pps)]

    @pl.when(c == 0)
    def _():
        qbd0 = _build_qbd(q_ref[...], HEAD_DIM ** -0.5)
        qbd_scr[...] = qbd0
        tn = max(tq, LANES)
        idx = lax.broadcasted_iota(jnp.int32, (1, tn), 1)
        allowed = (_expand_mask(mnew_ref[0].astype(bf16), idx, blk_shift) & (idx <= _row_token(rows, tq, tn))
                   & (idx < tq))
        s0 = jnp.where(allowed, _dot_nt(qbd0, _pad_rows(kn_ref[...], tn).astype(bf16)), NEG_INF)
        m0 = jnp.max(s0, axis=-1, keepdims=True)
        ones = jnp.where(lax.broadcasted_iota(jnp.int32, (tn, LANES), 1) == 0, 1.0, 0.0)
        vaug_n = jnp.concatenate([_pad_rows(vn_ref[...], tn).astype(bf16), ones.astype(bf16)], axis=1)
        m_scr[...] = m0
        acc_scr[...] = _dot(jnp.exp(s0 - m0).astype(bf16), vaug_n)

    unselected = (1.0 - mpast_ref[0, 0]).astype(bf16)
    k_t = jnp.concatenate([p[...] for p in kp], axis=1).astype(bf16)
    s = _dot(qbd_scr[...], k_t) + _dot(unselected, aux_ref[0:LANES, :])
    m_old = m_scr[...]
    m_new = jnp.maximum(m_old, jnp.max(s, axis=-1, keepdims=True))
    p = jnp.exp(s - m_new).astype(bf16)
    v_t = jnp.concatenate([p_[...] for p_ in vp], axis=1).astype(bf16)
    pv = jnp.concatenate([_dot_nt(p, v_t), _dot_nt(p, aux_ref[LANES:, :])], axis=1)
    acc = jnp.exp(m_old - m_new) * acc_scr[...] + pv
    m_scr[...] = m_new
    acc_scr[...] = acc

    @pl.when(c == nc - 1)
    def _():
        o_ref[...] = _unstack(acc[:, :LANES] / acc[:, LANES:LANES + 1], tq)


def _block_attn_sample(q, pool_k, pool_v, table, k_new, v_new, mask, nb, tq, blk):
    n, hq = q.shape
    page = pool_k.shape[2]
    n_pages = table.shape[0] // nb
    pps = min(PAGES_PER_STEP, n_pages)
    nc = n_pages // pps
    nbp = mask.shape[2]
    rows = N_HEADS * tq
    tk = pps * page
    bpc = tk // blk
    n_past_blk = nc * bpc
    assert bpc <= LANES and nbp - n_past_blk <= LANES
    m4 = mask.reshape(N_HEADS, nb, tq, nbp)
    mpast = m4[..., :n_past_blk].reshape(N_HEADS, nb, tq, nc, bpc)
    mpast = jnp.transpose(mpast, (1, 3, 0, 2, 4)).reshape(nb, nc, rows, bpc)
    mpast = jnp.pad(mpast, ((0, 0), (0, 0), (0, 0), (0, LANES - bpc)))
    mnew = jnp.transpose(m4[..., n_past_blk:], (1, 0, 2, 3)).reshape(nb, rows, nbp - n_past_blk)
    mnew = jnp.pad(mnew, ((0, 0), (0, 0), (0, LANES - (nbp - n_past_blk))))
    col_blk = jnp.arange(tk, dtype=jnp.int32)[None, :] // blk
    row = jnp.arange(LANES, dtype=jnp.int32)[:, None]
    aux = jnp.concatenate([jnp.where(row == col_blk, NEG_INF, 0.0),
                           jnp.where(row == 0, 1.0, 0.0) * jnp.ones((1, tk), f32)], axis=0).astype(bf16)
    cur = lambda b, c, tab: (b, 0)
    grid_spec = pltpu.PrefetchScalarGridSpec(
        num_scalar_prefetch=1,
        grid=(nb, nc),
        in_specs=[pl.BlockSpec(memory_space=pl.ANY), pl.BlockSpec(memory_space=pl.ANY),
                  pl.BlockSpec((tq, hq), cur), pl.BlockSpec((tq, LANES), cur), pl.BlockSpec((tq, LANES), cur),
                  pl.BlockSpec((1, 1, rows, LANES), lambda b, c, tab: (b, c, 0, 0)),
                  pl.BlockSpec((1, rows, LANES), lambda b, c, tab: (b, 0, 0)),
                  pl.BlockSpec((2 * LANES, tk), lambda b, c, tab: (0, 0))],
        out_specs=pl.BlockSpec((tq, hq), cur),
        scratch_shapes=[pltpu.VMEM((rows, LANES), bf16), pltpu.VMEM((rows, 1), f32),
                        pltpu.VMEM((rows, 2 * LANES), f32)] + _paged_scratch(2, pps, page),
    )
    return pl.pallas_call(
        functools.partial(_blk_sample_body, pps, nc, tq, _log2(blk)),
        grid_spec=grid_spec,
        out_shape=jax.ShapeDtypeStruct((n, hq), f32),
        compiler_params=_params(2),
        name="block_attn_sample",
    )(table, pool_k, pool_v, q, k_new, v_new, mpast, mnew, aux)


def _round_up(x, m):
    return -(-x // m) * m


def _rows_last(a):
    n, rows = a.shape[:2]
    return jnp.transpose(a, (0, 2, 3, 1)).reshape(n * N_KV * HEAD_DIM, rows)


def _forward(x, nb, t, p0, past, prm):
    n = nb * t
    prompt = past is None
    tm = min(FFN_ROWS, n)
    if prompt:
        pos = jnp.arange(t, dtype=jnp.int32)
        n_tab = t // tm
    else:
        pos = p0 + jnp.arange(tm, dtype=jnp.int32) % t
        n_tab = 1
    tables = _rope_tables(pos)

    x = _ffn(x, prm["norm_ffn_a"][0], *prm["ffn_a"][0])
    cdim = prm["l0_conv_w"].shape[1]
    hq, hk = N_HEADS * HEAD_DIM, N_KV * HEAD_DIM
    gb, cu, q, k, v = _proj(functools.partial(_proj_ab_body, cdim), x, prm["norm_mix"][0], prm["l0_w_in"],
                            tables, n_tab, (cdim, cdim, hq, hk, hk))
    if prompt:
        tq = min(ATTN_ROWS, t)
        y_attn, y_conv = _window_attn(q, k, v, None, None, nb, t // tq, tq, SWA_WINDOW,
                                      conv=(gb, cu, None, prm["l0_conv_w"]), sinks=prm["l0_sinks"])
    else:
        conv_buf = jnp.pad(past["conv"], ((0, 0), (SUBLANES - (CONV_WIDTH - 1), 0), (0, 0)))
        y_attn, y_conv = _window_attn(q, k, v, _rows_last(past["swa_k"]), _rows_last(past["swa_v"]),
                                      nb, 1, t, SWA_WINDOW,
                                      conv=(gb, cu, conv_buf.reshape(nb * SUBLANES, cdim), prm["l0_conv_w"]),
                                      sinks=prm["l0_sinks"])
    x = _ffn(x, prm["norm_ffn_b"][0], *prm["ffn_b"][0], mix=(y_conv, y_attn, prm["l0_w_out"]))
    states0 = (cu, k, v)

    x = _ffn(x, prm["norm_ffn_a"][1], *prm["ffn_a"][1])
    qm, km, vm, qn, kc, vc, ks, vs, kw, vw, gates = _proj(
        _proj_cd_body, x, prm["norm_mix"][1], prm["l1_w_in"], tables, n_tab,
        (hq, hk, hk, hq, hk, hk, hk, hk, hk, hk, LANES))
    if prompt:
        tq = min(ATTN_ROWS, t)
        nt = t // tq
        n_blk = t // MOBA_BLOCK
        nbp = _round_up(n_blk + 1, LANES)
        means = _means_prompt(km, nb, t, nbp)
        moba_mask = _moba_select(qm, means, nb, nt, tq, 0, nbp)
        o_moba = _block_attn_prompt(qm, km, vm, moba_mask, nb, nt, tq, MOBA_BLOCK)
        ck = _compress_prompt(kc, nb, t, prm["cmp_k"])
        cv = _compress_prompt(vc, nb, t, prm["cmp_v"])
        n_ck = t // NSA_CMP_STRIDE
        nsp = _round_up(-(-t // NSA_SEL_BLOCK), LANES)
        o_cmp, sel_mask = _cmp_select(qn, ck, cv, nb, nt, tq, 0, n_ck, n_ck - 1, nsp)
        o_sel = _block_attn_prompt(qn, ks, vs, sel_mask, nb, nt, tq, NSA_SEL_BLOCK)
        (o_win,) = _window_attn(qn, kw, vw, None, None, nb, nt, tq, NSA_WINDOW)
    else:
        table = past["table"]
        n_past = p0
        n_blk = n_past // MOBA_BLOCK
        nbp = _round_up(n_blk + 1, LANES)
        km_t = jnp.transpose(km.reshape(nb, t, hk), (0, 2, 1)).reshape(nb * hk, t)
        means = _means_sample(past["moba_k"], table, km_t, nb, t, nbp)
        moba_mask = _moba_select(qm, means, nb, 1, t, p0, nbp)
        o_moba = _block_attn_sample(qm, past["moba_k"], past["moba_v"], table, km, vm, moba_mask, nb, t,
                                    MOBA_BLOCK)
        ck = _compress_sample(past["cmp_k"], table, kc, nb, t, prm["cmp_k"])
        cv = _compress_sample(past["cmp_v"], table, vc, nb, t, prm["cmp_v"])
        n_ck = n_past // NSA_CMP_STRIDE
        nsp = _round_up(-(-(n_past + t) // NSA_SEL_BLOCK), LANES)
        o_cmp, sel_mask = _cmp_select(qn, ck, cv, nb, 1, t, p0, n_ck, n_ck, nsp)
        o_sel = _block_attn_sample(qn, past["sel_k"], past["sel_v"], table, ks, vs, sel_mask, nb, t,
                                   NSA_SEL_BLOCK)
        (o_win,) = _window_attn(qn, kw, vw, _rows_last(past["win_k"]), _rows_last(past["win_v"]),
                                nb, 1, t, NSA_WINDOW)
    x = _ffn(x, prm["norm_ffn_b"][1], *prm["ffn_b"][1], mix=(o_moba, o_cmp, o_sel, o_win, gates, prm["l1_w_out"]),
             final_g=prm["norm_final"])
    return x, states0 + (km, vm, kc, vc, ks, vs, kw, vw)


def _tail(buf, new, rows):
    full = new if buf is None else jnp.concatenate([buf.astype(new.dtype), new], axis=1)
    return full[:, full.shape[1] - rows:]


def kernel(x_prompt, x_sample, state_l0_conv, cache_l0_swa_k, cache_l0_swa_v, cache_l1_moba_k, cache_l1_moba_v, cache_l1_nsa_cmp_k, cache_l1_nsa_cmp_v, cache_l1_nsa_sel_k, cache_l1_nsa_sel_v, cache_l1_nsa_win_k, cache_l1_nsa_win_v, page_table, norm_ffn_a, ffn_a_gate, ffn_a_up, ffn_a_down, norm_mix, norm_ffn_b, ffn_b_gate, ffn_b_up, ffn_b_down, norm_final, l0_w_in, l0_conv_w, l0_sinks, l0_w_out, l1_w_in, l1_cmp_k_w1, l1_cmp_k_b1, l1_cmp_k_w2, l1_cmp_v_w1, l1_cmp_v_b1, l1_cmp_v_w2, l1_w_out):
    bp, tp, d = x_prompt.shape
    bs, ts, _ = x_sample.shape
    depth = norm_ffn_a.shape[0]
    assert depth == 2 and N_KV * HEAD_DIM == LANES
    n_pool, page = cache_l1_moba_k.shape[:2]
    p0 = page_table.shape[1] * page
    hk = N_KV * HEAD_DIM

    w1_pad = _round_up(l1_w_in.shape[1], LANES) - l1_w_in.shape[1]
    prm = {
        "norm_ffn_a": norm_ffn_a, "norm_mix": norm_mix, "norm_ffn_b": norm_ffn_b, "norm_final": norm_final,
        "ffn_a": [(ffn_a_gate[l].astype(bf16), ffn_a_up[l].astype(bf16), ffn_a_down[l].astype(bf16))
                  for l in range(depth)],
        "ffn_b": [(ffn_b_gate[l].astype(bf16), ffn_b_up[l].astype(bf16), ffn_b_down[l].astype(bf16))
                  for l in range(depth)],
        "l0_w_in": l0_w_in.astype(bf16), "l0_conv_w": l0_conv_w, "l0_sinks": l0_sinks,
        "l0_w_out": l0_w_out.astype(bf16),
        "l1_w_in": jnp.pad(l1_w_in, ((0, 0), (0, w1_pad))).astype(bf16), "l1_w_out": l1_w_out.astype(bf16),
        "cmp_k": _cmp_weights(l1_cmp_k_w1, l1_cmp_k_b1, l1_cmp_k_w2),
        "cmp_v": _cmp_weights(l1_cmp_v_w1, l1_cmp_v_b1, l1_cmp_v_w2),
    }
    pool = lambda a: _rows_last(a).reshape(n_pool, hk, page)
    past = {
        "conv": state_l0_conv, "swa_k": cache_l0_swa_k, "swa_v": cache_l0_swa_v,
        "moba_k": pool(cache_l1_moba_k), "moba_v": pool(cache_l1_moba_v),
        "cmp_k": pool(cache_l1_nsa_cmp_k), "cmp_v": pool(cache_l1_nsa_cmp_v),
        "sel_k": pool(cache_l1_nsa_sel_k), "sel_v": pool(cache_l1_nsa_sel_v),
        "win_k": cache_l1_nsa_win_k, "win_v": cache_l1_nsa_win_v,
        "table": page_table.reshape(-1),
    }
    y_p, st_p = _forward(x_prompt.reshape(bp * tp, d), bp, tp, 0, None, prm)
    y_s, st_s = _forward(x_sample.reshape(bs * ts, d), bs, ts, p0, past, prm)

    def states(st, nb, t, bufs):
        cu, k0, v0, km, vm, kc, vc, ks, vs, kw, vw = st
        kv = lambda a: a.reshape(nb, t, N_KV, HEAD_DIM)
        conv = _tail(bufs["conv"], cu.reshape(nb, t, -1), CONV_WIDTH - 1)
        swa_rows = min(SWA_WINDOW, t) if bufs["swa_k"] is None else bufs["swa_k"].shape[1]
        win_rows = min(NSA_WINDOW, t) if bufs["win_k"] is None else bufs["win_k"].shape[1]
        return (conv, _tail(bufs["swa_k"], kv(k0), swa_rows), _tail(bufs["swa_v"], kv(v0), swa_rows),
                kv(km), kv(vm), kv(kc), kv(vc), kv(ks), kv(vs),
                _tail(bufs["win_k"], kv(kw), win_rows), _tail(bufs["win_v"], kv(vw), win_rows))

    none = {"conv": None, "swa_k": None, "swa_v": None, "win_k": None, "win_v": None}
    sp = states(st_p, bp, tp, none)
    ss = states(st_s, bs, ts, past)
    conv_p, swa_k_p, swa_v_p, mk_p, mv_p, ck_p, cv_p, sk_p, sv_p, wk_p, wv_p = sp
    conv_s, swa_k_s, swa_v_s, mk_s, mv_s, ck_s, cv_s, sk_s, sv_s, wk_s, wv_s = ss
    return (y_p.reshape(bp, tp, d), y_s.reshape(bs, ts, d), conv_p, conv_s, swa_k_p, swa_v_p, swa_k_s, swa_v_s,
            mk_p, mv_p, mk_s, mv_s, ck_p, cv_p, ck_s, cv_s,
            sk_p, sv_p, sk_s, sv_s, wk_p, wv_p, wk_s, wv_s)
```

```python
import functools
import math

import jax
import jax.numpy as jnp
from jax import lax
from jax.experimental import pallas as pl
from jax.experimental.pallas import tpu as pltpu

f32 = jnp.float32
bf16 = jnp.bfloat16

HEAD_DIM = 64
N_HEADS = 8
N_KV = 2
ROPE_THETA = 10000.0
NORM_EPS = 1e-5
CONV_WIDTH = 3
SWA_WINDOW = 128
MOBA_BLOCK = 256
MOBA_TOPK = 3
NSA_CMP_LEN = 32
NSA_CMP_STRIDE = 16
NSA_SEL_BLOCK = 64
NSA_TOPN = 16
NSA_WINDOW = 512
NEG_INF = -1e30
FORCE_SCORE = 1e9

LANES = 128
SUBLANES = 8
VMEM_LIMIT_BYTES = 60 * 1024 * 1024

FFN_ROWS = 512
PAGES_PER_STEP = 64
ATTN_ROWS = 256
SOFTMAX_ROWS = 64

_ARB = "arbitrary"


def _params(n_axes):
    return pltpu.CompilerParams(dimension_semantics=(_ARB,) * n_axes, vmem_limit_bytes=VMEM_LIMIT_BYTES)


def _log2(n):
    assert n > 0 and n & (n - 1) == 0, n
    return n.bit_length() - 1


def _dot(a, b):
    return jnp.dot(a, b, preferred_element_type=f32)


def _dot_nt(a, b):
    return lax.dot_general(a, b, (((1,), (1,)), ((), ())), preferred_element_type=f32)


def _split3(x):
    hi = x.astype(bf16)
    r1 = x - hi.astype(f32)
    mid = r1.astype(bf16)
    lo = (r1 - mid.astype(f32)).astype(bf16)
    return hi, mid, lo


def _rms(x, g):
    return x * lax.rsqrt(jnp.mean(x * x, axis=-1, keepdims=True) + NORM_EPS) * g


def _gate_expand(gates):
    hq = N_HEADS * HEAD_DIM
    n = lax.broadcasted_iota(jnp.int32, (LANES, 3 * hq), 0)
    col = lax.broadcasted_iota(jnp.int32, (LANES, 3 * hq), 1)
    comp = jnp.right_shift(col, _log2(hq))
    head = jnp.right_shift(col & (hq - 1), _log2(HEAD_DIM))
    e = jnp.where(n == 3 * head + comp, 1.0, 0.0).astype(bf16)
    hi = gates.astype(bf16)
    lo = (gates - hi.astype(f32)).astype(bf16)
    return _dot(hi, e) + _dot(lo, e)


def _ffn_body(mode, final, nf, *refs):
    it = iter(refs)
    x_ref = next(it)
    if mode == 1:
        pa_ref, pb_ref, wo_ref = next(it), next(it), next(it)
    elif mode == 2:
        om_ref, oc_ref, os_ref, ow_ref, gt_ref, wo_ref = (next(it) for _ in range(6))
    g_ref, wg_ref, wu_ref, wd_ref = next(it), next(it), next(it), next(it)
    gf_ref = next(it) if final else None
    o_ref, h_scr, acc_scr = next(it), next(it), next(it)
    j = pl.program_id(1)

    @pl.when(j == 0)
    def _():
        x1 = x_ref[...]
        if mode == 1:
            half = pa_ref.shape[1]
            x1 = x1 + _dot(pa_ref[...].astype(bf16), wo_ref[0:half, :])
            x1 = x1 + _dot(pb_ref[...].astype(bf16), wo_ref[half:2 * half, :])
        elif mode == 2:
            half = om_ref.shape[1]
            g = _gate_expand(gt_ref[...])
            y_nsa = (g[:, 0:half] * oc_ref[...] + g[:, half:2 * half] * os_ref[...]
                     + g[:, 2 * half:3 * half] * ow_ref[...])
            x1 = x1 + _dot(om_ref[...].astype(bf16), wo_ref[0:half, :])
            x1 = x1 + _dot(y_nsa.astype(bf16), wo_ref[half:2 * half, :])
        acc_scr[...] = x1
        h_scr[...] = _rms(x1, g_ref[...]).astype(bf16)

    h = h_scr[...]
    gate = _dot(h, wg_ref[...])
    up = _dot(h, wu_ref[...])
    act = gate * (1.0 / (1.0 + jnp.exp(-gate))) * up
    acc_scr[...] += 0.5 * _dot(act.astype(bf16), wd_ref[...])

    @pl.when(j == nf - 1)
    def _():
        y = acc_scr[...]
        if final:
            y = _rms(y, gf_ref[...])
        o_ref[...] = y


def _ffn_chunk(d_ff):
    best = LANES
    for c in range(LANES, 1536 + 1, LANES):
        if d_ff % c == 0:
            best = c
    return best if d_ff % LANES == 0 else d_ff


def _ffn(x, g, wg, wu, wd, mix=None, final_g=None):
    n, d = x.shape
    d_ff = wg.shape[1]
    tm = min(FFN_ROWS, n)
    tf = _ffn_chunk(d_ff)
    nf = d_ff // tf
    row = lambda i, j: (i, 0)
    const = lambda i, j: (0, 0)
    args, specs = [x], [pl.BlockSpec((tm, d), row)]
    mode = 0
    if mix is not None:
        mode = 1 if len(mix) == 3 else 2
        for a in mix[:-1]:
            args.append(a)
            specs.append(pl.BlockSpec((tm, a.shape[1]), row))
        args.append(mix[-1])
        specs.append(pl.BlockSpec(mix[-1].shape, const))
    args += [g.reshape(1, d), wg, wu, wd]
    specs += [pl.BlockSpec((1, d), const), pl.BlockSpec((d, tf), lambda i, j: (0, j)),
              pl.BlockSpec((d, tf), lambda i, j: (0, j)), pl.BlockSpec((tf, d), lambda i, j: (j, 0))]
    if final_g is not None:
        args.append(final_g.reshape(1, d))
        specs.append(pl.BlockSpec((1, d), const))
    return pl.pallas_call(
        functools.partial(_ffn_body, mode, final_g is not None, nf),
        grid=(n // tm, nf),
        in_specs=specs,
        out_specs=pl.BlockSpec((tm, d), row),
        out_shape=jax.ShapeDtypeStruct((n, d), f32),
        scratch_shapes=[pltpu.VMEM((tm, d), bf16), pltpu.VMEM((tm, d), f32)],
        compiler_params=_params(2),
        name="ffn",
    )(*args)


def _rope_tables(pos):
    half = HEAD_DIM // 2
    inv_freq = ROPE_THETA ** (-jnp.arange(half, dtype=f32) * 2.0 / HEAD_DIM)
    ang = pos.astype(f32)[:, None] * inv_freq[None, :]
    cos, sin = jnp.cos(ang), jnp.sin(ang)
    zero = jnp.zeros_like(sin)
    reps = LANES // HEAD_DIM
    return (jnp.concatenate([cos, cos] * reps, axis=1), jnp.concatenate([-sin, zero] * reps, axis=1),
            jnp.concatenate([zero, sin] * reps, axis=1))


def _rope(x, cos, sa, sb):
    half = HEAD_DIM // 2
    outs = []
    for c in range(x.shape[1] // LANES):
        xc = x[:, c * LANES:(c + 1) * LANES]
        outs.append(xc * cos + pltpu.roll(xc, LANES - half, 1) * sa + pltpu.roll(xc, half, 1) * sb)
    return outs[0] if len(outs) == 1 else jnp.concatenate(outs, axis=1)


def _proj_ab_body(cdim, x_ref, g_ref, w_ref, cos_ref, sa_ref, sb_ref, gb_ref, cu_ref, q_ref, k_ref, v_ref):
    h = _rms(x_ref[...], g_ref[...]).astype(bf16)
    cos, sa, sb = cos_ref[...], sa_ref[...], sb_ref[...]
    seg = lambda off, width: _dot(h, w_ref[:, off:off + width])
    hq, hk = q_ref.shape[1], k_ref.shape[1]
    gb_ref[...] = seg(0, cdim)
    cu_ref[...] = seg(cdim, cdim) * seg(2 * cdim, cdim)
    q_ref[...] = _rope(seg(3 * cdim, hq), cos, sa, sb)
    k_ref[...] = _rope(seg(3 * cdim + hq, hk), cos, sa, sb)
    v_ref[...] = seg(3 * cdim + hq + hk, hk)


def _proj_cd_body(x_ref, g_ref, w_ref, cos_ref, sa_ref, sb_ref,
                  qm_ref, km_ref, vm_ref, qn_ref, kc_ref, vc_ref, ks_ref, vs_ref, kw_ref, vw_ref, gt_ref):
    h = _rms(x_ref[...], g_ref[...]).astype(bf16)
    cos, sa, sb = cos_ref[...], sa_ref[...], sb_ref[...]
    off = 0
    for ref, kind in ((qm_ref, "rope"), (km_ref, "rope"), (vm_ref, ""), (qn_ref, "rope"), (kc_ref, ""),
                      (vc_ref, ""), (ks_ref, "rope"), (vs_ref, ""), (kw_ref, "rope"), (vw_ref, ""),
                      (gt_ref, "sigmoid")):
        width = ref.shape[1]
        z = _dot(h, w_ref[:, off:off + width])
        if kind == "rope":
            z = _rope(z, cos, sa, sb)
        elif kind == "sigmoid":
            z = 1.0 / (1.0 + jnp.exp(-z))
        ref[...] = z
        off += width


def _proj(body, x, g, w, tables, n_tab_tiles, widths):
    n, d = x.shape
    tm = min(FFN_ROWS, n)
    row = lambda i: (i, 0)
    const = lambda i: (0, 0)
    tab = lambda i: (i % n_tab_tiles, 0)
    return pl.pallas_call(
        body,
        grid=(n // tm,),
        in_specs=[pl.BlockSpec((tm, d), row), pl.BlockSpec((1, d), const), pl.BlockSpec(w.shape, const)]
        + [pl.BlockSpec((tm, LANES), tab)] * 3,
        out_specs=[pl.BlockSpec((tm, wd), row) for wd in widths],
        out_shape=[jax.ShapeDtypeStruct((n, wd), f32) for wd in widths],
        compiler_params=_params(1),
        name="proj",
    )(x, g.reshape(1, d), w, *tables)


def _build_qbd(q, scale):
    tq = q.shape[0]
    lo = lax.broadcasted_iota(jnp.int32, (tq, LANES), 1) < HEAD_DIM
    per = N_HEADS // N_KV
    pieces = []
    for h in range(N_HEADS):
        sl = q[:, LANES * (h // 2):LANES * (h // 2) + LANES]
        src_hi, dst_hi = (h % 2 == 1), (h // per == 1)
        if src_hi != dst_hi:
            sl = pltpu.roll(sl, HEAD_DIM, 1)
        pieces.append(jnp.where(lo != dst_hi, sl, 0.0))
    return (jnp.concatenate(pieces, axis=0) * scale).astype(bf16)


def _unstack(o, tq):
    lo = lax.broadcasted_iota(jnp.int32, (tq, LANES), 1) < HEAD_DIM
    per = N_HEADS // N_KV
    outs = []
    for c in range(N_HEADS // 2):
        a = o[(2 * c) * tq:(2 * c + 1) * tq]
        b = o[(2 * c + 1) * tq:(2 * c + 2) * tq]
        if (2 * c) // per == 1:
            a = pltpu.roll(a, HEAD_DIM, 1)
        if (2 * c + 1) // per == 0:
            b = pltpu.roll(b, HEAD_DIM, 1)
        outs.append(jnp.where(lo, a, b))
    return jnp.concatenate(outs, axis=1)


def _row_token(rows, tq, width):
    _log2(tq)
    return lax.broadcasted_iota(jnp.int32, (rows, width), 0) & (tq - 1)


def _pad_rows(x, rows):
    if x.shape[0] >= rows:
        return x
    return jnp.concatenate([x, jnp.zeros((rows - x.shape[0], x.shape[1]), x.dtype)], axis=0)


def _expand_mask(m_rows, key_pos, blk_shift):
    nbp, tk = m_rows.shape[1], key_pos.shape[1]
    blk = jnp.right_shift(key_pos, blk_shift)
    e = jnp.where(lax.broadcasted_iota(jnp.int32, (nbp, tk), 0) == blk, 1.0, 0.0).astype(bf16)
    return _dot(m_rows, e) > 0.5


def _win_body(tq, window, hb, nh, hist_always, with_conv, with_sinks, *refs):
    it = iter(refs)
    q_ref, kc_ref, vc_ref = (next(it) for _ in range(3))
    kh_refs = [next(it) for _ in range(nh)]
    vh_refs = [next(it) for _ in range(nh)]
    band_ref = next(it)
    if with_conv:
        gb_ref, cu_ref, cup_ref, cw_ref = (next(it) for _ in range(4))
    sink_ref = next(it) if with_sinks else None
    o_ref = next(it)
    oc_ref = next(it) if with_conv else None
    s_scr, p_scr, bias_scr, d_scr = (next(it) for _ in range(4))
    i = pl.program_id(1)
    rows = N_HEADS * tq
    ktot = band_ref.shape[1]
    tk = max(tq, LANES)
    rb = min(rows, SOFTMAX_ROWS if ktot <= 4 * LANES else SOFTMAX_ROWS // 2)
    log2e = math.log2(math.e)

    qbd = _build_qbd(q_ref[...], HEAD_DIM ** -0.5 * log2e)
    k_parts = [r[...].astype(bf16) for r in kh_refs] + [_pad_rows(kc_ref[...], tk).astype(bf16)]
    v_parts = [r[...].astype(bf16) for r in vh_refs] + [_pad_rows(vc_ref[...], tk).astype(bf16)]
    offs = [j * hb for j in range(nh)] + [nh * hb]
    widths = [hb] * nh + [tk]
    flipped = [hist_always] * nh + [False]
    for kp, off, w, fl in zip(k_parts, offs, widths, flipped):
        s_scr[:, off:off + w] = _dot(qbd, kp) if fl else _dot_nt(qbd, kp)

    if hist_always:
        bias_src = band_ref
    else:
        col = lax.broadcasted_iota(jnp.int32, (1, ktot), 1)
        colbias = jnp.zeros((1, ktot), f32)
        for j in range(nh):
            missing = (col >= j * hb) & (col < (j + 1) * hb) & (i * (tq // hb) < nh - j)
            colbias = jnp.where(missing, NEG_INF, colbias)
        bias_scr[...] = band_ref[...] + colbias
        bias_src = bias_scr

    for r in range(rows // rb):
        sl = slice(r * rb, (r + 1) * rb)
        if rb <= tq:
            t0 = (r * rb) % tq
            bias = bias_src[t0:t0 + rb, :]
        else:
            bias = jnp.concatenate([bias_src[...]] * (rb // tq), axis=0)
        s = s_scr[sl, :] + bias
        m = jnp.max(s, axis=-1, keepdims=True)
        p_scr[sl, :] = jnp.exp2(s - m).astype(bf16)
        if with_sinks:
            if tq % rb == 0:
                sink = sink_ref[(r * rb) // tq]
            else:
                head = jnp.right_shift(r * rb + lax.broadcasted_iota(jnp.int32, (rb, 1), 0), _log2(tq))
                sink = jnp.zeros((rb, 1), f32)
                for h in range(N_HEADS):
                    sink = jnp.where(head == h, sink_ref[h], sink)
            d_scr[sl, :] = jnp.broadcast_to(jnp.exp2(sink * log2e - m), (rb, LANES))

    p_all = p_scr[...]
    acc = jnp.zeros((rows, LANES), f32)
    for vp, off, w, fl in zip(v_parts, offs, widths, flipped):
        acc = acc + (_dot_nt(p_all[:, off:off + w], vp) if fl else _dot(p_all[:, off:off + w], vp))
    den = _dot(p_all, jnp.ones((ktot, LANES), bf16))
    if with_sinks:
        den = den + d_scr[...]
    o_ref[...] = _unstack(acc / den, tq)
    hist_ok = True if hist_always else (i > 0)

    if with_conv:
        cu = cu_ref[...]
        prev = jnp.where(hist_ok, cup_ref[...], 0.0)
        r = lax.broadcasted_iota(jnp.int32, cu.shape, 0)
        p1 = prev[SUBLANES - 1:SUBLANES, :]
        p2 = prev[SUBLANES - 2:SUBLANES - 1, :]
        cu_m1 = jnp.where(r == 0, p1, pltpu.roll(cu, 1, 0))
        cu_m2 = jnp.where(r == 0, p2, jnp.where(r == 1, p1, pltpu.roll(cu, 2, 0)))
        cw = cw_ref[...]
        oc_ref[...] = gb_ref[...] * (cu_m2 * cw[0:1, :] + cu_m1 * cw[1:2, :] + cu * cw[2:3, :])


def _window_attn(q, k, v, k_hist, v_hist, nb, nt, tq, window, conv=None, sinks=None):
    n = q.shape[0]
    hq = q.shape[1]
    per_seq = nt * tq
    cur = lambda b, i: (b * nt + i, 0)
    if k_hist is None:
        hist_always = False
        hb = min(window, tq)
        nh = window // hb
        assert tq % hb == 0 and window % hb == 0

        def hist_map(j):
            return lambda b, i: (b * (per_seq // hb) + jnp.maximum(i * (tq // hb) - (nh - j), 0), 0)

        k_hist, v_hist = k, v
    else:
        hist_always = True
        hb, nh = window, 1
        hist_map = lambda j: (lambda b, i: (b, 0))
    hist_block = (LANES, hb) if hist_always else (hb, LANES)
    tk = max(tq, LANES)
    t_col = jnp.arange(tq, dtype=jnp.int32)[:, None]
    rel_h = jnp.arange(nh * hb, dtype=jnp.int32)[None, :] - nh * hb
    c_own = jnp.arange(tk, dtype=jnp.int32)[None, :]
    ok = jnp.concatenate([rel_h >= t_col - window,
                          (c_own <= t_col) & (c_own >= t_col - window) & (c_own < tq)], axis=1)
    band = jnp.where(ok, 0.0, NEG_INF).astype(f32)
    ktot = nh * hb + tk
    rows = N_HEADS * tq
    args = [q, k, v] + [k_hist] * nh + [v_hist] * nh + [band]
    specs = ([pl.BlockSpec((tq, hq), cur), pl.BlockSpec((tq, LANES), cur), pl.BlockSpec((tq, LANES), cur)]
             + [pl.BlockSpec(hist_block, hist_map(j)) for j in range(nh)] * 2
             + [pl.BlockSpec((tq, ktot), lambda b, i: (0, 0))])
    out_shape = [jax.ShapeDtypeStruct((n, hq), f32)]
    out_specs = [pl.BlockSpec((tq, hq), cur)]
    if conv is not None:
        gb, cu, cu_prev, cw = conv
        cdim = gb.shape[1]
        if cu_prev is None:
            prev = lambda b, i: (jnp.maximum((b * per_seq + i * tq) // SUBLANES - 1, 0), 0)
            cu_prev = cu
        else:
            prev = lambda b, i: (b, 0)
        args += [gb, cu, cu_prev, cw]
        specs += [pl.BlockSpec((tq, cdim), cur), pl.BlockSpec((tq, cdim), cur),
                  pl.BlockSpec((SUBLANES, cdim), prev), pl.BlockSpec(cw.shape, lambda b, i: (0, 0))]
        out_shape.append(jax.ShapeDtypeStruct((n, cdim), f32))
        out_specs.append(pl.BlockSpec((tq, cdim), cur))
    if sinks is not None:
        args.append(sinks)
        specs.append(pl.BlockSpec(memory_space=pltpu.SMEM))
    return pl.pallas_call(
        functools.partial(_win_body, tq, window, hb, nh, hist_always, conv is not None, sinks is not None),
        grid=(nb, nt),
        in_specs=specs,
        out_specs=out_specs,
        out_shape=out_shape,
        scratch_shapes=[pltpu.VMEM((rows, ktot), f32), pltpu.VMEM((rows, ktot), bf16),
                        pltpu.VMEM((tq, ktot), f32), pltpu.VMEM((rows, LANES), f32)],
        compiler_params=_params(2),
        name="window_attn",
    )(*args)


def _cmp_weights(w1, b1, w2):
    n_half = NSA_CMP_LEN // NSA_CMP_STRIDE
    hid = w1.shape[2]
    w1h = w1.reshape(n_half, NSA_CMP_STRIDE, HEAD_DIM, hid)
    big = jnp.zeros((NSA_CMP_STRIDE, N_KV * HEAD_DIM, n_half * N_KV * hid), f32)
    for hh in range(n_half):
        for g in range(N_KV):
            big = big.at[:, g * HEAD_DIM:(g + 1) * HEAD_DIM,
                         hh * N_KV * hid + g * hid:hh * N_KV * hid + (g + 1) * hid].set(w1h[hh])
    w2bd = jnp.zeros((N_KV * hid, N_KV * HEAD_DIM), f32)
    for g in range(N_KV):
        w2bd = w2bd.at[g * hid:(g + 1) * hid, g * HEAD_DIM:(g + 1) * HEAD_DIM].set(w2)
    return big.astype(bf16), jnp.tile(b1, N_KV).reshape(1, N_KV * hid), w2bd.astype(bf16)


def _cmp_finish(hp, n_tok, b1, w2bd):
    rows = hp.shape[0]
    hp1 = pltpu.roll(hp[:, LANES:], rows - 1, 0)
    hid = jax.nn.gelu(hp[:, :LANES] + hp1 + b1)
    return _dot(hid.astype(bf16), w2bd)[:n_tok]


def _cmp_prompt_body(n_chunk, x_ref, w1_ref, b1_ref, w2_ref, o_ref):
    hp = jnp.zeros((n_chunk, 2 * LANES), f32)
    for s in range(NSA_CMP_STRIDE):
        hp = hp + _dot(x_ref[pl.ds(s, n_chunk, stride=NSA_CMP_STRIDE), :].astype(bf16), w1_ref[s])
    r = lax.broadcasted_iota(jnp.int32, (n_chunk, LANES), 0)
    hp1 = jnp.where(r == n_chunk - 1, 0.0, pltpu.roll(hp[:, LANES:], n_chunk - 1, 0))
    hid = jax.nn.gelu(hp[:, :LANES] + hp1 + b1_ref[...])
    o_ref[...] = _dot(hid.astype(bf16), w2_ref[...])


def _compress_prompt(x, nb, t, wts):
    w1, b1, w2 = wts
    n_chunk = t // NSA_CMP_STRIDE
    c3 = lambda b: (0, 0, 0)
    c2 = lambda b: (0, 0)
    return pl.pallas_call(
        functools.partial(_cmp_prompt_body, n_chunk),
        grid=(nb,),
        in_specs=[pl.BlockSpec((t, LANES), lambda b: (b, 0)), pl.BlockSpec(w1.shape, c3),
                  pl.BlockSpec(b1.shape, c2), pl.BlockSpec(w2.shape, c2)],
        out_specs=pl.BlockSpec((n_chunk, LANES), lambda b: (b, 0)),
        out_shape=jax.ShapeDtypeStruct((nb * n_chunk, LANES), f32),
        compiler_params=_params(1),
        name="compress_prompt",
    )(x, w1, b1, w2)


def _paged_scratch(n_pools, pps, page):
    return ([pltpu.VMEM((2, pps, LANES, page), f32) for _ in range(n_pools)]
            + [pltpu.SemaphoreType.DMA((n_pools, 2))])


def _page_copy(pool, buf, sem, k, slot, src_page, i):
    return pltpu.make_async_copy(pool.at[src_page], buf.at[slot, i], sem.at[k, slot])


def _paged_step(pools, bufs, sem, tab_ref, pps):
    nc = pl.num_programs(1)
    s = pl.program_id(0) * nc + pl.program_id(1)
    total = pl.num_programs(0) * nc
    slot = lax.rem(s, 2)

    def fetch(step, sl):
        def start(i, carry):
            src_page = tab_ref[step * pps + i]
            for k, (pool, buf) in enumerate(zip(pools, bufs)):
                _page_copy(pool, buf, sem, k, sl, src_page, i).start()
            return carry
        lax.fori_loop(0, pps, start, 0)

    @pl.when(s == 0)
    def _():
        fetch(0, 0)

    @pl.when(s + 1 < total)
    def _():
        fetch(s + 1, 1 - slot)

    for k, (pool, buf) in enumerate(zip(pools, bufs)):
        pltpu.make_async_copy(pool.at[pl.ds(0, pps)], buf.at[slot], sem.at[k, slot]).wait()
    return slot


def _cmp_sample_body(pps, nc, page, n_past_chunk, tab_ref, pool_ref, xn_ref, w1_ref, b1_ref, w2_ref, o_ref,
                     hp_scr, buf, sem):
    c = pl.program_id(1)
    slot = _paged_step([pool_ref], [buf], sem, tab_ref, pps)
    pages = [buf.at[slot, i] for i in range(pps)]
    cpp = page // NSA_CMP_STRIDE
    out_r = lax.broadcasted_iota(jnp.int32, (page, page), 0)
    src_r = lax.broadcasted_iota(jnp.int32, (page, page), 1)
    src_of = (out_r & (cpp - 1)) * NSA_CMP_STRIDE + jnp.right_shift(out_r, _log2(cpp))
    perm = jnp.where(src_r == src_of, 1.0, 0.0).astype(bf16)
    regrouped = [_dot_nt(perm, p[...].astype(bf16)) for p in pages]
    flat = jnp.concatenate([jnp.concatenate([x[s * cpp:(s + 1) * cpp] for x in regrouped], axis=0)
                            for s in range(NSA_CMP_STRIDE)], axis=1)
    hp = _dot(flat.astype(bf16), w1_ref[...].reshape(NSA_CMP_STRIDE * LANES, 2 * LANES))
    hp_scr[pl.ds(pl.multiple_of(c * (pps * cpp), SUBLANES), pps * cpp), :] = hp

    @pl.when(c == nc - 1)
    def _():
        xn = xn_ref[...].astype(bf16)
        r = lax.broadcasted_iota(jnp.int32, (NSA_CMP_STRIDE, 2 * LANES), 0)
        new = jnp.zeros((NSA_CMP_STRIDE, 2 * LANES), f32)
        for s in range(NSA_CMP_STRIDE):
            new = new + jnp.where(r == s, _dot(xn, w1_ref[s]), 0.0)
        new = jnp.sum(new, axis=0, keepdims=True)
        r8 = lax.broadcasted_iota(jnp.int32, (SUBLANES, 2 * LANES), 0)
        hp_scr[n_past_chunk:n_past_chunk + SUBLANES, :] = jnp.where(r8 == 0, new, 0.0)
        o_ref[...] = _cmp_finish(hp_scr[...], n_past_chunk, b1_ref[...], w2_ref[...])


def _compress_sample(pool, table, x_new, nb, t_new, wts):
    w1, b1, w2 = wts
    page = pool.shape[2]
    n_pages = table.shape[0] // nb
    pps = min(PAGES_PER_STEP, n_pages)
    nc = n_pages // pps
    n_past_chunk = n_pages * page // NSA_CMP_STRIDE
    assert t_new <= NSA_CMP_STRIDE
    xn = jnp.pad(x_new.reshape(nb, t_new, LANES), ((0, 0), (0, NSA_CMP_STRIDE - t_new), (0, 0)))
    xn = xn.reshape(nb * NSA_CMP_STRIDE, LANES)
    c3 = lambda b, c, tab: (0, 0, 0)
    c2 = lambda b, c, tab: (0, 0)
    grid_spec = pltpu.PrefetchScalarGridSpec(
        num_scalar_prefetch=1,
        grid=(nb, nc),
        in_specs=[pl.BlockSpec(memory_space=pl.ANY),
                  pl.BlockSpec((NSA_CMP_STRIDE, LANES), lambda b, c, tab: (b, 0)), pl.BlockSpec(w1.shape, c3),
                  pl.BlockSpec(b1.shape, c2), pl.BlockSpec(w2.shape, c2)],
        out_specs=pl.BlockSpec((n_past_chunk, LANES), lambda b, c, tab: (b, 0)),
        scratch_shapes=[pltpu.VMEM((n_past_chunk + SUBLANES, 2 * LANES), f32)] + _paged_scratch(1, pps, page),
    )
    return pl.pallas_call(
        functools.partial(_cmp_sample_body, pps, nc, page, n_past_chunk),
        grid_spec=grid_spec,
        out_shape=jax.ShapeDtypeStruct((nb * n_past_chunk, LANES), f32),
        compiler_params=_params(2),
        name="compress_sample",
    )(table, pool, xn, w1, b1, w2)


def _means_prompt_body(n_blk, nbp, k_ref, o_ref):
    t = k_ref.shape[0]
    sums = jnp.sum(k_ref[...].reshape(n_blk, t // n_blk, LANES), axis=1)
    o_ref[...] = _pad_rows(sums * (1.0 / MOBA_BLOCK), nbp).T


def _means_prompt(k, nb, t, nbp):
    n_blk = t // MOBA_BLOCK
    return pl.pallas_call(
        functools.partial(_means_prompt_body, n_blk, nbp),
        grid=(nb,),
        in_specs=[pl.BlockSpec((t, LANES), lambda b: (b, 0))],
        out_specs=pl.BlockSpec((LANES, nbp), lambda b: (b, 0)),
        out_shape=jax.ShapeDtypeStruct((nb * LANES, nbp), f32),
        compiler_params=_params(1),
        name="moba_means_prompt",
    )(k)


def _means_sample_body(pps, nc, n_past_blk, nbp, tab_ref, pool_ref, kn_ref, o_ref, buf, sem):
    c = pl.program_id(1)
    slot = _paged_step([pool_ref], [buf], sem, tab_ref, pps)
    pages = [buf.at[slot, i] for i in range(pps)]
    per = pps // 2

    lane = lax.broadcasted_iota(jnp.int32, (LANES, nbp), 1)
    @pl.when(c == 0)
    def _():
        o_ref[...] = jnp.zeros((LANES, nbp), f32)

    acc = o_ref[...]
    for i in range(per):
        col = jnp.sum(pages[2 * i][...] + pages[2 * i + 1][...], axis=1, keepdims=True) * (1.0 / MOBA_BLOCK)
        acc = jnp.where(lane == c * per + i, col, acc)
    new = jnp.sum(kn_ref[...], axis=1, keepdims=True) * (1.0 / MOBA_BLOCK)
    acc = jnp.where((lane == n_past_blk) & (c == nc - 1), new, acc)
    o_ref[...] = acc


def _means_sample(pool, table, k_new_t, nb, t_new, nbp):
    page = pool.shape[2]
    n_pages = table.shape[0] // nb
    pps = min(PAGES_PER_STEP, n_pages)
    nc = n_pages // pps
    assert MOBA_BLOCK == 2 * page and t_new <= MOBA_BLOCK and pps % 2 == 0
    n_past_blk = n_pages * page // MOBA_BLOCK
    grid_spec = pltpu.PrefetchScalarGridSpec(
        num_scalar_prefetch=1,
        grid=(nb, nc),
        in_specs=[pl.BlockSpec(memory_space=pl.ANY), pl.BlockSpec((LANES, t_new), lambda b, c, tab: (b, 0))],
        out_specs=pl.BlockSpec((LANES, nbp), lambda b, c, tab: (b, 0)),
        scratch_shapes=_paged_scratch(1, pps, page),
    )
    return pl.pallas_call(
        functools.partial(_means_sample_body, pps, nc, n_past_blk, nbp),
        grid_spec=grid_spec,
        out_shape=jax.ShapeDtypeStruct((nb * LANES, nbp), f32),
        compiler_params=_params(2),
        name="moba_means_sample",
    )(table, pool, k_new_t)


def _topk_cols(score, k):
    n = score.shape[0]
    ridx = lax.broadcasted_iota(jnp.int32, score.shape, 0)
    sel = jnp.zeros(score.shape, f32)
    work = score
    for _ in range(k):
        mx = jnp.max(work, axis=0, keepdims=True)
        cand = (work == mx) & (mx > -jnp.inf)
        first = jnp.min(jnp.where(cand, ridx, n), axis=0, keepdims=True)
        pick = ridx == first
        sel = jnp.where(pick, 1.0, sel)
        work = jnp.where(pick, -jnp.inf, work)
    return sel


def _moba_sel_body(tq, p0, nbp, q_ref, mean_ref, o_ref):
    i = pl.program_id(1)
    rows = N_HEADS * tq
    cols = max(rows, LANES)
    qbd = _pad_rows(_build_qbd(q_ref[...], 1.0).astype(f32), cols)
    q_hi = qbd.astype(bf16)
    q_lo = (qbd - q_hi.astype(f32)).astype(bf16)
    mean = mean_ref[...]
    m_hi = mean.astype(bf16)
    m_lo = (mean - m_hi.astype(f32)).astype(bf16)
    s = (_dot(q_hi, m_hi) + _dot(q_lo, m_hi) + _dot(q_hi, m_lo)).T
    n_idx = lax.broadcasted_iota(jnp.int32, (nbp, cols), 0)
    t = lax.broadcasted_iota(jnp.int32, (nbp, cols), 1) & (tq - 1)
    cur = jnp.right_shift(p0 + i * tq + t, _log2(MOBA_BLOCK))
    sel = _topk_cols(jnp.where(n_idx < cur, s, -jnp.inf), MOBA_TOPK)
    sel = jnp.where(n_idx == cur, 1.0, sel)
    sel_t = sel.T
    for h in range(N_HEADS):
        o_ref[h] = sel_t[h * tq:(h + 1) * tq]


def _moba_select(q, means, nb, nt, tq, p0, nbp):
    n, hq = q.shape
    return pl.pallas_call(
        functools.partial(_moba_sel_body, tq, p0, nbp),
        grid=(nb, nt),
        in_specs=[pl.BlockSpec((tq, hq), lambda b, i: (b * nt + i, 0)),
                  pl.BlockSpec((LANES, nbp), lambda b, i: (b, 0))],
        out_specs=pl.BlockSpec((N_HEADS, tq, nbp), lambda b, i: (0, b * nt + i, 0)),
        out_shape=jax.ShapeDtypeStruct((N_HEADS, n, nbp), f32),
        compiler_params=_params(2),
        name="moba_select",
    )(q, means)


def _cmp_sel_body(tq, p0, n_ck, n_cmp, nsp, q_ref, ck_ref, cv_ref, o_ref, m_ref):
    i = pl.program_id(1)
    rows = N_HEADS * tq
    per = N_HEADS // N_KV
    qbd = _build_qbd(q_ref[...], HEAD_DIM ** -0.5)
    n_idx = lax.broadcasted_iota(jnp.int32, (rows, n_ck), 1)
    pos = p0 + i * tq + _row_token(rows, tq, n_ck)
    valid = (n_idx * NSA_CMP_STRIDE + (NSA_CMP_LEN - 1) <= pos) & (n_idx < n_cmp)
    s = jnp.where(valid, _dot_nt(qbd, ck_ref[...].astype(bf16)), NEG_INF)
    mx = jnp.max(s, axis=-1, keepdims=True)
    e = jnp.where(valid, jnp.exp(s - mx), 0.0)
    den = jnp.sum(e, axis=-1, keepdims=True)
    p = e * jnp.where(den > 0.0, 1.0 / den, 0.0)
    o_ref[...] = _unstack(_dot(p.astype(bf16), cv_ref[...].astype(bf16)), tq)

    cols = max(tq, LANES)
    j_idx = lax.broadcasted_iota(jnp.int32, (nsp, n_ck), 0)
    n_of = lax.broadcasted_iota(jnp.int32, (nsp, n_ck), 1)
    ratio = NSA_SEL_BLOCK // NSA_CMP_STRIDE
    m_span = NSA_CMP_LEN // NSA_CMP_STRIDE
    lo = ratio * j_idx - (m_span - 1)
    a_t = jnp.where((n_of >= lo) & (n_of <= lo + ratio + m_span - 2), 1.0, 0.0).astype(bf16)
    j_col = lax.broadcasted_iota(jnp.int32, (nsp, N_KV * cols), 0)
    t_col = lax.broadcasted_iota(jnp.int32, (nsp, N_KV * cols), 1) & (cols - 1)
    j_cur = jnp.right_shift(p0 + i * tq + t_col, _log2(NSA_SEL_BLOCK))
    forced = (j_col == 0) | (j_col == j_cur) | (j_col == j_cur - 1)
    p_slc = []
    for g in range(N_KV):
        imp = p[(g * per) * tq:(g * per + 1) * tq]
        for r in range(1, per):
            imp = imp + p[(g * per + r) * tq:(g * per + r + 1) * tq]
        imp = _pad_rows(imp, cols)
        hi, mid, lo3 = _split3(imp)
        p_slc.append(_dot_nt(a_t, hi) + _dot_nt(a_t, mid) + _dot_nt(a_t, lo3))
    score = jnp.where(forced, FORCE_SCORE, jnp.where(j_col <= j_cur, jnp.concatenate(p_slc, axis=1), -jnp.inf))
    sel = _topk_cols(score, NSA_TOPN)
    for g in range(N_KV):
        sel_t = sel[:, g * cols:(g + 1) * cols].T[:tq]
        for r in range(per):
            m_ref[g * per + r] = sel_t


def _cmp_select(q, ck, cv, nb, nt, tq, p0, n_ck, n_cmp, nsp):
    n, hq = q.shape
    cur = lambda b, i: (b * nt + i, 0)
    return pl.pallas_call(
        functools.partial(_cmp_sel_body, tq, p0, n_ck, n_cmp, nsp),
        grid=(nb, nt),
        in_specs=[pl.BlockSpec((tq, hq), cur), pl.BlockSpec((n_ck, LANES), lambda b, i: (b, 0)),
                  pl.BlockSpec((n_ck, LANES), lambda b, i: (b, 0))],
        out_specs=[pl.BlockSpec((tq, hq), cur),
                   pl.BlockSpec((N_HEADS, tq, nsp), lambda b, i: (0, b * nt + i, 0))],
        out_shape=[jax.ShapeDtypeStruct((n, hq), f32), jax.ShapeDtypeStruct((N_HEADS, n, nsp), f32)],
        compiler_params=_params(2),
        name="cmp_select",
    )(q, ck, cv)


def _blk_prompt_body(tq, blk_shift, q_ref, k_ref, v_ref, mask_ref, o_ref,
                     kaug_scr, vaug_scr, qaug_scr, s_scr, p_scr, a_scr, m_scr, acc_scr):
    i = pl.program_id(1)
    rows = N_HEADS * tq
    nbp = mask_ref.shape[2]
    t_len = k_ref.shape[0]

    @pl.when(i == 0)
    def _():
        key = lax.broadcasted_iota(jnp.int32, (nbp, t_len), 1)
        blk = lax.broadcasted_iota(jnp.int32, (nbp, t_len), 0)
        kaug_scr[:LANES, :] = k_ref[...].T.astype(bf16)
        kaug_scr[LANES:, :] = jnp.where(blk == jnp.right_shift(key, blk_shift), NEG_INF, 0.0).astype(bf16)
        ones = jnp.where(lax.broadcasted_iota(jnp.int32, (t_len, LANES), 1) == 0, 1.0, 0.0)
        vaug_scr[:, :LANES] = v_ref[...].astype(bf16)
        vaug_scr[:, LANES:] = ones.astype(bf16)

    unselected = 1.0 - mask_ref[...].reshape(rows, nbp)
    qbd = _build_qbd(q_ref[...], HEAD_DIM ** -0.5 * math.log2(math.e))
    qaug_scr[...] = jnp.concatenate([qbd, unselected.astype(bf16)], axis=1)
    m_scr[...] = jnp.full((rows, LANES), NEG_INF, f32)
    acc_scr[...] = jnp.zeros((rows, 2 * LANES), f32)

    def tile(j, causal):
        st = pl.multiple_of(j * tq, tq)
        s_scr[...] = _dot(qaug_scr[...], kaug_scr[:, pl.ds(st, tq)])
        for r in range(rows // SOFTMAX_ROWS):
            sl = slice(r * SOFTMAX_ROWS, (r + 1) * SOFTMAX_ROWS)
            s = s_scr[sl, :]
            if causal:
                t_row = (r * SOFTMAX_ROWS + lax.broadcasted_iota(jnp.int32, (SOFTMAX_ROWS, tq), 0)) & (tq - 1)
                s = jnp.where(lax.broadcasted_iota(jnp.int32, (SOFTMAX_ROWS, tq), 1) <= t_row, s, NEG_INF)
            m_old = m_scr[sl, :]
            m_new = jnp.maximum(m_old, jnp.max(s, axis=-1, keepdims=True))
            p_scr[sl, :] = jnp.exp2(s - jnp.concatenate([m_new] * (tq // LANES), axis=1)).astype(bf16)
            a_scr[sl, :] = jnp.exp2(m_old - m_new)
            m_scr[sl, :] = m_new
        a = a_scr[...]
        acc_scr[...] = jnp.concatenate([a, a], axis=1) * acc_scr[...] + _dot(p_scr[...], vaug_scr[pl.ds(st, tq), :])

    tile(i, True)
    lax.fori_loop(0, i, lambda j, carry: (tile(j, False), carry)[1], 0)
    acc = acc_scr[...]
    o_ref[...] = _unstack(acc[:, :LANES] / acc[:, LANES:LANES + 1], tq)


def _block_attn_prompt(q, k, v, mask, nb, nt, tq, blk):
    n, hq = q.shape
    t = nt * tq
    nbp = mask.shape[2]
    assert nbp == LANES
    rows = N_HEADS * tq
    cur = lambda b, i: (b * nt + i, 0)
    seq = lambda b, i: (b, 0)
    return pl.pallas_call(
        functools.partial(_blk_prompt_body, tq, _log2(blk)),
        grid=(nb, nt),
        in_specs=[pl.BlockSpec((tq, hq), cur), pl.BlockSpec((t, LANES), seq), pl.BlockSpec((t, LANES), seq),
                  pl.BlockSpec((N_HEADS, tq, nbp), lambda b, i: (0, b * nt + i, 0))],
        out_specs=pl.BlockSpec((tq, hq), cur),
        out_shape=jax.ShapeDtypeStruct((n, hq), f32),
        scratch_shapes=[pltpu.VMEM((2 * LANES, t), bf16), pltpu.VMEM((t, 2 * LANES), bf16),
                        pltpu.VMEM((rows, 2 * LANES), bf16), pltpu.VMEM((rows, tq), f32),
                        pltpu.VMEM((rows, tq), bf16), pltpu.VMEM((rows, LANES), f32),
                        pltpu.VMEM((rows, LANES), f32), pltpu.VMEM((rows, 2 * LANES), f32)],
        compiler_params=_params(2),
        name="block_attn_prompt",
    )(q, k, v, mask)


def _blk_sample_body(pps, nc, tq, blk_shift, tab_ref, kpool_ref, vpool_ref, q_ref, kn_ref, vn_ref, mpast_ref,
                     mnew_ref, aux_ref, o_ref, qbd_scr, m_scr, acc_scr, kbuf, vbuf, sem):
    c = pl.program_id(1)
    rows = N_HEADS * tq
    slot = _paged_step([kpool_ref, vpool_ref], [kbuf, vbuf], sem, tab_ref, pps)
    kp = [kbuf.at[slot, i] for i in range(pps)]
    vp = [vbuf.at[slot, i] for i in range(pps)]

    @pl.when(c == 0)
    def _():
        qbd0 = _build_qbd(q_ref[...], HEAD_DIM ** -0.5)
        qbd_scr[...] = qbd0
        tn = max(tq, LANES)
        idx = lax.broadcasted_iota(jnp.int32, (1, tn), 1)
        allowed = (_expand_mask(mnew_ref[0].astype(bf16), idx, blk_shift) & (idx <= _row_token(rows, tq, tn))
                   & (idx < tq))
        s0 = jnp.where(allowed, _dot_nt(qbd0, _pad_rows(kn_ref[...], tn).astype(bf16)), NEG_INF)
        m0 = jnp.max(s0, axis=-1, keepdims=True)
        ones = jnp.where(lax.broadcasted_iota(jnp.int32, (tn, LANES), 1) == 0, 1.0, 0.0)
        vaug_n = jnp.concatenate([_pad_rows(vn_ref[...], tn).astype(bf16), ones.astype(bf16)], axis=1)
        m_scr[...] = m0
        acc_scr[...] = _dot(jnp.exp(s0 - m0).astype(bf16), vaug_n)

    unselected = (1.0 - mpast_ref[0, 0]).astype(bf16)
    k_t = jnp.concatenate([p[...] for p in kp], axis=1).astype(bf16)
    s = _dot(qbd_scr[...], k_t) + _dot(unselected, aux_ref[0:LANES, :])
    m_old = m_scr[...]
    m_new = jnp.maximum(m_old, jnp.max(s, axis=-1, keepdims=True))
    p = jnp.exp(s - m_new).astype(bf16)
    v_t = jnp.concatenate([p_[...] for p_ in vp], axis=1).astype(bf16)
    pv = jnp.concatenate([_dot_nt(p, v_t), _dot_nt(p, aux_ref[LANES:, :])], axis=1)
    acc = jnp.exp(m_old - m_new) * acc_scr[...] + pv
    m_scr[...] = m_new
    acc_scr[...] = acc

    @pl.when(c == nc - 1)
    def _():
        o_ref[...] = _unstack(acc[:, :LANES] / acc[:, LANES:LANES + 1], tq)


def _block_attn_sample(q, pool_k, pool_v, table, k_new, v_new, mask, nb, tq, blk):
    n, hq = q.shape
    page = pool_k.shape[2]
    n_pages = table.shape[0] // nb
    pps = min(PAGES_PER_STEP, n_pages)
    nc = n_pages // pps
    nbp = mask.shape[2]
    rows = N_HEADS * tq
    tk = pps * page
    bpc = tk // blk
    n_past_blk = nc * bpc
    assert bpc <= LANES and nbp - n_past_blk <= LANES
    m4 = mask.reshape(N_HEADS, nb, tq, nbp)
    mpast = m4[..., :n_past_blk].reshape(N_HEADS, nb, tq, nc, bpc)
    mpast = jnp.transpose(mpast, (1, 3, 0, 2, 4)).reshape(nb, nc, rows, bpc)
    mpast = jnp.pad(mpast, ((0, 0), (0, 0), (0, 0), (0, LANES - bpc)))
    mnew = jnp.transpose(m4[..., n_past_blk:], (1, 0, 2, 3)).reshape(nb, rows, nbp - n_past_blk)
    mnew = jnp.pad(mnew, ((0, 0), (0, 0), (0, LANES - (nbp - n_past_blk))))
    col_blk = jnp.arange(tk, dtype=jnp.int32)[None, :] // blk
    row = jnp.arange(LANES, dtype=jnp.int32)[:, None]
    aux = jnp.concatenate([jnp.where(row == col_blk, NEG_INF, 0.0),
                           jnp.where(row == 0, 1.0, 0.0) * jnp.ones((1, tk), f32)], axis=0).astype(bf16)
    cur = lambda b, c, tab: (b, 0)
    grid_spec = pltpu.PrefetchScalarGridSpec(
        num_scalar_prefetch=1,
        grid=(nb, nc),
        in_specs=[pl.BlockSpec(memory_space=pl.ANY), pl.BlockSpec(memory_space=pl.ANY),
                  pl.BlockSpec((tq, hq), cur), pl.BlockSpec((tq, LANES), cur), pl.BlockSpec((tq, LANES), cur),
                  pl.BlockSpec((1, 1, rows, LANES), lambda b, c, tab: (b, c, 0, 0)),
                  pl.BlockSpec((1, rows, LANES), lambda b, c, tab: (b, 0, 0)),
                  pl.BlockSpec((2 * LANES, tk), lambda b, c, tab: (0, 0))],
        out_specs=pl.BlockSpec((tq, hq), cur),
        scratch_shapes=[pltpu.VMEM((rows, LANES), bf16), pltpu.VMEM((rows, 1), f32),
                        pltpu.VMEM((rows, 2 * LANES), f32)] + _paged_scratch(2, pps, page),
    )
    return pl.pallas_call(
        functools.partial(_blk_sample_body, pps, nc, tq, _log2(blk)),
        grid_spec=grid_spec,
        out_shape=jax.ShapeDtypeStruct((n, hq), f32),
        compiler_params=_params(2),
        name="block_attn_sample",
    )(table, pool_k, pool_v, q, k_new, v_new, mpast, mnew, aux)


def _round_up(x, m):
    return -(-x // m) * m


def _rows_last(a):
    n, rows = a.shape[:2]
    return jnp.transpose(a, (0, 2, 3, 1)).reshape(n * N_KV * HEAD_DIM, rows)


def _forward(x, nb, t, p0, past, prm):
    n = nb * t
    prompt = past is None
    tm = min(FFN_ROWS, n)
    if prompt:
        pos = jnp.arange(t, dtype=jnp.int32)
        n_tab = t // tm
    else:
        pos = p0 + jnp.arange(tm, dtype=jnp.int32) % t
        n_tab = 1
    tables = _rope_tables(pos)

    x = _ffn(x, prm["norm_ffn_a"][0], *prm["ffn_a"][0])
    cdim = prm["l0_conv_w"].shape[1]
    hq, hk = N_HEADS * HEAD_DIM, N_KV * HEAD_DIM
    gb, cu, q, k, v = _proj(functools.partial(_proj_ab_body, cdim), x, prm["norm_mix"][0], prm["l0_w_in"],
                            tables, n_tab, (cdim, cdim, hq, hk, hk))
    if prompt:
        tq = min(ATTN_ROWS, t)
        y_attn, y_conv = _window_attn(q, k, v, None, None, nb, t // tq, tq, SWA_WINDOW,
                                      conv=(gb, cu, None, prm["l0_conv_w"]), sinks=prm["l0_sinks"])
    else:
        conv_buf = jnp.pad(past["conv"], ((0, 0), (SUBLANES - (CONV_WIDTH - 1), 0), (0, 0)))
        y_attn, y_conv = _window_attn(q, k, v, _rows_last(past["swa_k"]), _rows_last(past["swa_v"]),
                                      nb, 1, t, SWA_WINDOW,
                                      conv=(gb, cu, conv_buf.reshape(nb * SUBLANES, cdim), prm["l0_conv_w"]),
                                      sinks=prm["l0_sinks"])
    x = _ffn(x, prm["norm_ffn_b"][0], *prm["ffn_b"][0], mix=(y_conv, y_attn, prm["l0_w_out"]))
    states0 = (cu, k, v)

    x = _ffn(x, prm["norm_ffn_a"][1], *prm["ffn_a"][1])
    qm, km, vm, qn, kc, vc, ks, vs, kw, vw, gates = _proj(
        _proj_cd_body, x, prm["norm_mix"][1], prm["l1_w_in"], tables, n_tab,
        (hq, hk, hk, hq, hk, hk, hk, hk, hk, hk, LANES))
    if prompt:
        tq = min(ATTN_ROWS, t)
        nt = t // tq
        n_blk = t // MOBA_BLOCK
        nbp = _round_up(n_blk + 1, LANES)
        means = _means_prompt(km, nb, t, nbp)
        moba_mask = _moba_select(qm, means, nb, nt, tq, 0, nbp)
        o_moba = _block_attn_prompt(qm, km, vm, moba_mask, nb, nt, tq, MOBA_BLOCK)
        ck = _compress_prompt(kc, nb, t, prm["cmp_k"])
        cv = _compress_prompt(vc, nb, t, prm["cmp_v"])
        n_ck = t // NSA_CMP_STRIDE
        nsp = _round_up(-(-t // NSA_SEL_BLOCK), LANES)
        o_cmp, sel_mask = _cmp_select(qn, ck, cv, nb, nt, tq, 0, n_ck, n_ck - 1, nsp)
        o_sel = _block_attn_prompt(qn, ks, vs, sel_mask, nb, nt, tq, NSA_SEL_BLOCK)
        (o_win,) = _window_attn(qn, kw, vw, None, None, nb, nt, tq, NSA_WINDOW)
    else:
        table = past["table"]
        n_past = p0
        n_blk = n_past // MOBA_BLOCK
        nbp = _round_up(n_blk + 1, LANES)
        km_t = jnp.transpose(km.reshape(nb, t, hk), (0, 2, 1)).reshape(nb * hk, t)
        means = _means_sample(past["moba_k"], table, km_t, nb, t, nbp)
        moba_mask = _moba_select(qm, means, nb, 1, t, p0, nbp)
        o_moba = _block_attn_sample(qm, past["moba_k"], past["moba_v"], table, km, vm, moba_mask, nb, t,
                                    MOBA_BLOCK)
        ck = _compress_sample(past["cmp_k"], table, kc, nb, t, prm["cmp_k"])
        cv = _compress_sample(past["cmp_v"], table, vc, nb, t, prm["cmp_v"])
        n_ck = n_past // NSA_CMP_STRIDE
        nsp = _round_up(-(-(n_past + t) // NSA_SEL_BLOCK), LANES)
        o_cmp, sel_mask = _cmp_select(qn, ck, cv, nb, 1, t, p0, n_ck, n_ck, nsp)
        o_sel = _block_attn_sample(qn, past["sel_k"], past["sel_v"], table, ks, vs, sel_mask, nb, t,
                                   NSA_SEL_BLOCK)
        (o_win,) = _window_attn(qn, kw, vw, _rows_last(past["win_k"]), _rows_last(past["win_v"]),
                                nb, 1, t, NSA_WINDOW)
    x = _ffn(x, prm["norm_ffn_b"][1], *prm["ffn_b"][1], mix=(o_moba, o_cmp, o_sel, o_win, gates, prm["l1_w_out"]),
             final_g=prm["norm_final"])
    return x, states0 + (km, vm, kc, vc, ks, vs, kw, vw)


def _tail(buf, new, rows):
    full = new if buf is None else jnp.concatenate([buf.astype(new.dtype), new], axis=1)
    return full[:, full.shape[1] - rows:]


def kernel(x_prompt, x_sample, state_l0_conv, cache_l0_swa_k, cache_l0_swa_v, cache_l1_moba_k, cache_l1_moba_v, cache_l1_nsa_cmp_k, cache_l1_nsa_cmp_v, cache_l1_nsa_sel_k, cache_l1_nsa_sel_v, cache_l1_nsa_win_k, cache_l1_nsa_win_v, page_table, norm_ffn_a, ffn_a_gate, ffn_a_up, ffn_a_down, norm_mix, norm_ffn_b, ffn_b_gate, ffn_b_up, ffn_b_down, norm_final, l0_w_in, l0_conv_w, l0_sinks, l0_w_out, l1_w_in, l1_cmp_k_w1, l1_cmp_k_b1, l1_cmp_k_w2, l1_cmp_v_w1, l1_cmp_v_b1, l1_cmp_v_w2, l1_w_out):
    bp, tp, d = x_prompt.shape
    bs, ts, _ = x_sample.shape
    depth = norm_ffn_a.shape[0]
    assert depth == 2 and N_KV * HEAD_DIM == LANES
    n_pool, page = cache_l1_moba_k.shape[:2]
    p0 = page_table.shape[1] * page
    hk = N_KV * HEAD_DIM

    w1_pad = _round_up(l1_w_in.shape[1], LANES) - l1_w_in.shape[1]
    prm = {
        "norm_ffn_a": norm_ffn_a, "norm_mix": norm_mix, "norm_ffn_b": norm_ffn_b, "norm_final": norm_final,
        "ffn_a": [(ffn_a_gate[l].astype(bf16), ffn_a_up[l].astype(bf16), ffn_a_down[l].astype(bf16))
                  for l in range(depth)],
        "ffn_b": [(ffn_b_gate[l].astype(bf16), ffn_b_up[l].astype(bf16), ffn_b_down[l].astype(bf16))
                  for l in range(depth)],
        "l0_w_in": l0_w_in.astype(bf16), "l0_conv_w": l0_conv_w, "l0_sinks": l0_sinks,
        "l0_w_out": l0_w_out.astype(bf16),
        "l1_w_in": jnp.pad(l1_w_in, ((0, 0), (0, w1_pad))).astype(bf16), "l1_w_out": l1_w_out.astype(bf16),
        "cmp_k": _cmp_weights(l1_cmp_k_w1, l1_cmp_k_b1, l1_cmp_k_w2),
        "cmp_v": _cmp_weights(l1_cmp_v_w1, l1_cmp_v_b1, l1_cmp_v_w2),
    }
    pool = lambda a: _rows_last(a).reshape(n_pool, hk, page)
    past = {
        "conv": state_l0_conv, "swa_k": cache_l0_swa_k, "swa_v": cache_l0_swa_v,
        "moba_k": pool(cache_l1_moba_k), "moba_v": pool(cache_l1_moba_v),
        "cmp_k": pool(cache_l1_nsa_cmp_k), "cmp_v": pool(cache_l1_nsa_cmp_v),
        "sel_k": pool(cache_l1_nsa_sel_k), "sel_v": pool(cache_l1_nsa_sel_v),
        "win_k": cache_l1_nsa_win_k, "win_v": cache_l1_nsa_win_v,
        "table": page_table.reshape(-1),
    }
    y_p, st_p = _forward(x_prompt.reshape(bp * tp, d), bp, tp, 0, None, prm)
    y_s, st_s = _forward(x_sample.reshape(bs * ts, d), bs, ts, p0, past, prm)

    def states(st, nb, t, bufs):
        cu, k0, v0, km, vm, kc, vc, ks, vs, kw, vw = st
        kv = lambda a: a.reshape(nb, t, N_KV, HEAD_DIM)
        conv = _tail(bufs["conv"], cu.reshape(nb, t, -1), CONV_WIDTH - 1)
        swa_rows = min(SWA_WINDOW, t) if bufs["swa_k"] is None else bufs["swa_k"].shape[1]
        win_rows = min(NSA_WINDOW, t) if bufs["win_k"] is None else bufs["win_k"].shape[1]
        return (conv, _tail(bufs["swa_k"], kv(k0), swa_rows), _tail(bufs["swa_v"], kv(v0), swa_rows),
                kv(km), kv(vm), kv(kc), kv(vc), kv(ks), kv(vs),
                _tail(bufs["win_k"], kv(kw), win_rows), _tail(bufs["win_v"], kv(vw), win_rows))

    none = {"conv": None, "swa_k": None, "swa_v": None, "win_k": None, "win_v": None}
    sp = states(st_p, bp, tp, none)
    ss = states(st_s, bs, ts, past)
    conv_p, swa_k_p, swa_v_p, mk_p, mv_p, ck_p, cv_p, sk_p, sv_p, wk_p, wv_p = sp
    conv_s, swa_k_s, swa_v_s, mk_s, mv_s, ck_s, cv_s, sk_s, sv_s, wk_s, wv_s = ss
    return (y_p.reshape(bp, tp, d), y_s.reshape(bs, ts, d), conv_p, conv_s, swa_k_p, swa_v_p, swa_k_s, swa_v_s,
            mk_p, mv_p, mk_s, mv_s, ck_p, cv_p, ck_s, cv_s,
            sk_p, sv_p, sk_s, sv_s, wk_p, wv_p, wk_s, wv_s)
```

```python
import functools
import math

import jax
import jax.numpy as jnp
from jax import lax
from jax.experimental import pallas as pl
from jax.experimental.pallas import tpu as pltpu

f32 = jnp.float32
bf16 = jnp.bfloat16

HEAD_DIM = 64
N_HEADS = 8
N_KV = 2
ROPE_THETA = 10000.0
NORM_EPS = 1e-5
CONV_WIDTH = 3
SWA_WINDOW = 128
MOBA_BLOCK = 256
MOBA_TOPK = 3
NSA_CMP_LEN = 32
NSA_CMP_STRIDE = 16
NSA_SEL_BLOCK = 64
NSA_TOPN = 16
NSA_WINDOW = 512
NEG_INF = -1e30
FORCE_SCORE = 1e9

LANES = 128
SUBLANES = 8
VMEM_LIMIT_BYTES = 60 * 1024 * 1024

FFN_ROWS = 512
PAGES_PER_STEP = 64
ATTN_ROWS = 256
SOFTMAX_ROWS = 64

_ARB = "arbitrary"


def _params(n_axes):
    return pltpu.CompilerParams(dimension_semantics=(_ARB,) * n_axes, vmem_limit_bytes=VMEM_LIMIT_BYTES)


def _log2(n):
    assert n > 0 and n & (n - 1) == 0, n
    return n.bit_length() - 1


def _dot(a, b):
    return jnp.dot(a, b, preferred_element_type=f32)


def _dot_nt(a, b):
    return lax.dot_general(a, b, (((1,), (1,)), ((), ())), preferred_element_type=f32)


def _split3(x):
    hi = x.astype(bf16)
    r1 = x - hi.astype(f32)
    mid = r1.astype(bf16)
    lo = (r1 - mid.astype(f32)).astype(bf16)
    return hi, mid, lo


def _rms(x, g):
    return x * lax.rsqrt(jnp.mean(x * x, axis=-1, keepdims=True) + NORM_EPS) * g


def _gate_expand(gates):
    hq = N_HEADS * HEAD_DIM
    n = lax.broadcasted_iota(jnp.int32, (LANES, 3 * hq), 0)
    col = lax.broadcasted_iota(jnp.int32, (LANES, 3 * hq), 1)
    comp = jnp.right_shift(col, _log2(hq))
    head = jnp.right_shift(col & (hq - 1), _log2(HEAD_DIM))
    e = jnp.where(n == 3 * head + comp, 1.0, 0.0).astype(bf16)
    hi = gates.astype(bf16)
    lo = (gates - hi.astype(f32)).astype(bf16)
    return _dot(hi, e) + _dot(lo, e)


def _ffn_body(mode, final, nf, *refs):
    it = iter(refs)
    x_ref = next(it)
    if mode == 1:
        pa_ref, pb_ref, wo_ref = next(it), next(it), next(it)
    elif mode == 2:
        om_ref, oc_ref, os_ref, ow_ref, gt_ref, wo_ref = (next(it) for _ in range(6))
    g_ref, wg_ref, wu_ref, wd_ref = next(it), next(it), next(it), next(it)
    gf_ref = next(it) if final else None
    o_ref, h_scr, acc_scr = next(it), next(it), next(it)
    j = pl.program_id(1)

    @pl.when(j == 0)
    def _():
        x1 = x_ref[...]
        if mode == 1:
            half = pa_ref.shape[1]
            x1 = x1 + _dot(pa_ref[...].astype(bf16), wo_ref[0:half, :])
            x1 = x1 + _dot(pb_ref[...].astype(bf16), wo_ref[half:2 * half, :])
        elif mode == 2:
            half = om_ref.shape[1]
            g = _gate_expand(gt_ref[...])
            y_nsa = (g[:, 0:half] * oc_ref[...] + g[:, half:2 * half] * os_ref[...]
                     + g[:, 2 * half:3 * half] * ow_ref[...])
            x1 = x1 + _dot(om_ref[...].astype(bf16), wo_ref[0:half, :])
            x1 = x1 + _dot(y_nsa.astype(bf16), wo_ref[half:2 * half, :])
        acc_scr[...] = x1
        h_scr[...] = _rms(x1, g_ref[...]).astype(bf16)

    h = h_scr[...]
    gate = _dot(h, wg_ref[...])
    up = _dot(h, wu_ref[...])
    act = gate * (1.0 / (1.0 + jnp.exp(-gate))) * up
    acc_scr[...] += 0.5 * _dot(act.astype(bf16), wd_ref[...])

    @pl.when(j == nf - 1)
    def _():
        y = acc_scr[...]
        if final:
            y = _rms(y, gf_ref[...])
        o_ref[...] = y


def _ffn_chunk(d_ff):
    best = LANES
    for c in range(LANES, 1536 + 1, LANES):
        if d_ff % c == 0:
            best = c
    return best if d_ff % LANES == 0 else d_ff


def _ffn(x, g, wg, wu, wd, mix=None, final_g=None):
    n, d = x.shape
    d_ff = wg.shape[1]
    tm = min(FFN_ROWS, n)
    tf = _ffn_chunk(d_ff)
    nf = d_ff // tf
    row = lambda i, j: (i, 0)
    const = lambda i, j: (0, 0)
    args, specs = [x], [pl.BlockSpec((tm, d), row)]
    mode = 0
    if mix is not None:
        mode = 1 if len(mix) == 3 else 2
        for a in mix[:-1]:
            args.append(a)
            specs.append(pl.BlockSpec((tm, a.shape[1]), row))
        args.append(mix[-1])
        specs.append(pl.BlockSpec(mix[-1].shape, const))
    args += [g.reshape(1, d), wg, wu, wd]
    specs += [pl.BlockSpec((1, d), const), pl.BlockSpec((d, tf), lambda i, j: (0, j)),
              pl.BlockSpec((d, tf), lambda i, j: (0, j)), pl.BlockSpec((tf, d), lambda i, j: (j, 0))]
    if final_g is not None:
        args.append(final_g.reshape(1, d))
        specs.append(pl.BlockSpec((1, d), const))
    return pl.pallas_call(
        functools.partial(_ffn_body, mode, final_g is not None, nf),
        grid=(n // tm, nf),
        in_specs=specs,
        out_specs=pl.BlockSpec((tm, d), row),
        out_shape=jax.ShapeDtypeStruct((n, d), f32),
        scratch_shapes=[pltpu.VMEM((tm, d), bf16), pltpu.VMEM((tm, d), f32)],
        compiler_params=_params(2),
        name="ffn",
    )(*args)


def _rope_tables(pos):
    half = HEAD_DIM // 2
    inv_freq = ROPE_THETA ** (-jnp.arange(half, dtype=f32) * 2.0 / HEAD_DIM)
    ang = pos.astype(f32)[:, None] * inv_freq[None, :]
    cos, sin = jnp.cos(ang), jnp.sin(ang)
    zero = jnp.zeros_like(sin)
    reps = LANES // HEAD_DIM
    return (jnp.concatenate([cos, cos] * reps, axis=1), jnp.concatenate([-sin, zero] * reps, axis=1),
            jnp.concatenate([zero, sin] * reps, axis=1))


def _rope(x, cos, sa, sb):
    half = HEAD_DIM // 2
    outs = []
    for c in range(x.shape[1] // LANES):
        xc = x[:, c * LANES:(c + 1) * LANES]
        outs.append(xc * cos + pltpu.roll(xc, LANES - half, 1) * sa + pltpu.roll(xc, half, 1) * sb)
    return outs[0] if len(outs) == 1 else jnp.concatenate(outs, axis=1)


def _proj_ab_body(cdim, x_ref, g_ref, w_ref, cos_ref, sa_ref, sb_ref, gb_ref, cu_ref, q_ref, k_ref, v_ref):
    h = _rms(x_ref[...], g_ref[...]).astype(bf16)
    cos, sa, sb = cos_ref[...], sa_ref[...], sb_ref[...]
    seg = lambda off, width: _dot(h, w_ref[:, off:off + width])
    hq, hk = q_ref.shape[1], k_ref.shape[1]
    gb_ref[...] = seg(0, cdim)
    cu_ref[...] = seg(cdim, cdim) * seg(2 * cdim, cdim)
    q_ref[...] = _rope(seg(3 * cdim, hq), cos, sa, sb)
    k_ref[...] = _rope(seg(3 * cdim + hq, hk), cos, sa, sb)
    v_ref[...] = seg(3 * cdim + hq + hk, hk)


def _proj_cd_body(x_ref, g_ref, w_ref, cos_ref, sa_ref, sb_ref,
                  qm_ref, km_ref, vm_ref, qn_ref, kc_ref, vc_ref, ks_ref, vs_ref, kw_ref, vw_ref, gt_ref):
    h = _rms(x_ref[...], g_ref[...]).astype(bf16)
    cos, sa, sb = cos_ref[...], sa_ref[...], sb_ref[...]
    off = 0
    for ref, kind in ((qm_ref, "rope"), (km_ref, "rope"), (vm_ref, ""), (qn_ref, "rope"), (kc_ref, ""),
                      (vc_ref, ""), (ks_ref, "rope"), (vs_ref, ""), (kw_ref, "rope"), (vw_ref, ""),
                      (gt_ref, "sigmoid")):
        width = ref.shape[1]
        z = _dot(h, w_ref[:, off:off + width])
        if kind == "rope":
            z = _rope(z, cos, sa, sb)
        elif kind == "sigmoid":
            z = 1.0 / (1.0 + jnp.exp(-z))
        ref[...] = z
        off += width


def _proj_t_body(rope_flags, x_ref, g_ref, wt_ref, cos_ref, sin_ref, *o_refs):
    h = _rms(x_ref[...], g_ref[...]).astype(bf16)
    cos, sin = cos_ref[...], sin_ref[...]
    half = HEAD_DIM // 2
    for k, (o_ref, rope) in enumerate(zip(o_refs, rope_flags)):
        z = _dot_nt(wt_ref[k * LANES:(k + 1) * LANES, :], h)
        if rope:
            parts = []
            for g in range(N_KV):
                a = z[g * HEAD_DIM:g * HEAD_DIM + half]
                b = z[g * HEAD_DIM + half:(g + 1) * HEAD_DIM]
                parts += [a * cos - b * sin, b * cos + a * sin]
            z = jnp.concatenate(parts, axis=0)
        o_ref[...] = z


def _proj_t(x, g, wt, cos_t, sin_t, nb, t, rope_flags):
    n, d = x.shape
    tm = min(FFN_ROWS, t)
    ntt = t // tm
    half = HEAD_DIM // 2
    out_map = lambda i: (i // ntt, i % ntt)
    return pl.pallas_call(
        functools.partial(_proj_t_body, rope_flags),
        grid=(n // tm,),
        in_specs=[pl.BlockSpec((tm, d), lambda i: (i, 0)), pl.BlockSpec((1, d), lambda i: (0, 0)),
                  pl.BlockSpec(wt.shape, lambda i: (0, 0)),
                  pl.BlockSpec((half, tm), lambda i: (0, i % ntt)), pl.BlockSpec((half, tm), lambda i: (0, i % ntt))],
        out_specs=[pl.BlockSpec((LANES, tm), out_map) for _ in rope_flags],
        out_shape=[jax.ShapeDtypeStruct((nb * LANES, t), f32) for _ in rope_flags],
        compiler_params=_params(1),
        name="proj_rows_last",
    )(x, g.reshape(1, d), wt, cos_t, sin_t)


def _proj(body, x, g, w, tables, n_tab_tiles, widths):
    n, d = x.shape
    tm = min(FFN_ROWS, n)
    row = lambda i: (i, 0)
    const = lambda i: (0, 0)
    tab = lambda i: (i % n_tab_tiles, 0)
    return pl.pallas_call(
        body,
        grid=(n // tm,),
        in_specs=[pl.BlockSpec((tm, d), row), pl.BlockSpec((1, d), const), pl.BlockSpec(w.shape, const)]
        + [pl.BlockSpec((tm, LANES), tab)] * 3,
        out_specs=[pl.BlockSpec((tm, wd), row) for wd in widths],
        out_shape=[jax.ShapeDtypeStruct((n, wd), f32) for wd in widths],
        compiler_params=_params(1),
        name="proj",
    )(x, g.reshape(1, d), w, *tables)


def _build_qbd(q, scale):
    tq = q.shape[0]
    lo = lax.broadcasted_iota(jnp.int32, (tq, LANES), 1) < HEAD_DIM
    per = N_HEADS // N_KV
    pieces = []
    for h in range(N_HEADS):
        sl = q[:, LANES * (h // 2):LANES * (h // 2) + LANES]
        src_hi, dst_hi = (h % 2 == 1), (h // per == 1)
        if src_hi != dst_hi:
            sl = pltpu.roll(sl, HEAD_DIM, 1)
        pieces.append(jnp.where(lo != dst_hi, sl, 0.0))
    return (jnp.concatenate(pieces, axis=0) * scale).astype(bf16)


def _unstack(o, tq):
    lo = lax.broadcasted_iota(jnp.int32, (tq, LANES), 1) < HEAD_DIM
    per = N_HEADS // N_KV
    outs = []
    for c in range(N_HEADS // 2):
        a = o[(2 * c) * tq:(2 * c + 1) * tq]
        b = o[(2 * c + 1) * tq:(2 * c + 2) * tq]
        if (2 * c) // per == 1:
            a = pltpu.roll(a, HEAD_DIM, 1)
        if (2 * c + 1) // per == 0:
            b = pltpu.roll(b, HEAD_DIM, 1)
        outs.append(jnp.where(lo, a, b))
    return jnp.concatenate(outs, axis=1)


def _row_token(rows, tq, width):
    _log2(tq)
    return lax.broadcasted_iota(jnp.int32, (rows, width), 0) & (tq - 1)


def _pad_rows(x, rows):
    if x.shape[0] >= rows:
        return x
    return jnp.concatenate([x, jnp.zeros((rows - x.shape[0], x.shape[1]), x.dtype)], axis=0)


def _expand_mask(m_rows, key_pos, blk_shift):
    nbp, tk = m_rows.shape[1], key_pos.shape[1]
    blk = jnp.right_shift(key_pos, blk_shift)
    e = jnp.where(lax.broadcasted_iota(jnp.int32, (nbp, tk), 0) == blk, 1.0, 0.0).astype(bf16)
    return _dot(m_rows, e) > 0.5


def _win_body(tq, window, hb, nh, hist_always, with_conv, with_sinks, *refs):
    it = iter(refs)
    q_ref, kc_ref, vc_ref = (next(it) for _ in range(3))
    kh_refs = [next(it) for _ in range(nh)]
    vh_refs = [next(it) for _ in range(nh)]
    band_ref = next(it)
    if with_conv:
        gb_ref, cu_ref, cup_ref, cw_ref = (next(it) for _ in range(4))
    sink_ref = next(it) if with_sinks else None
    o_ref = next(it)
    oc_ref = next(it) if with_conv else None
    s_scr, p_scr, bias_scr, d_scr = (next(it) for _ in range(4))
    i = pl.program_id(1)
    rows = N_HEADS * tq
    ktot = band_ref.shape[1]
    tk = max(tq, LANES)
    rb = min(rows, SOFTMAX_ROWS if ktot <= 4 * LANES else SOFTMAX_ROWS // 2)
    log2e = math.log2(math.e)

    qbd = _build_qbd(q_ref[...], HEAD_DIM ** -0.5 * log2e)
    k_parts = [r[...].astype(bf16) for r in kh_refs] + [_pad_rows(kc_ref[...], tk).astype(bf16)]
    v_parts = [r[...].astype(bf16) for r in vh_refs] + [_pad_rows(vc_ref[...], tk).astype(bf16)]
    offs = [j * hb for j in range(nh)] + [nh * hb]
    widths = [hb] * nh + [tk]
    flipped = [hist_always] * nh + [False]
    for kp, off, w, fl in zip(k_parts, offs, widths, flipped):
        s_scr[:, off:off + w] = _dot(qbd, kp) if fl else _dot_nt(qbd, kp)

    if hist_always:
        bias_src = band_ref
    else:
        col = lax.broadcasted_iota(jnp.int32, (1, ktot), 1)
        colbias = jnp.zeros((1, ktot), f32)
        for j in range(nh):
            missing = (col >= j * hb) & (col < (j + 1) * hb) & (i * (tq // hb) < nh - j)
            colbias = jnp.where(missing, NEG_INF, colbias)
        bias_scr[...] = band_ref[...] + colbias
        bias_src = bias_scr

    for r in range(rows // rb):
        sl = slice(r * rb, (r + 1) * rb)
        if rb <= tq:
            t0 = (r * rb) % tq
            bias = bias_src[t0:t0 + rb, :]
        else:
            bias = jnp.concatenate([bias_src[...]] * (rb // tq), axis=0)
        s = s_scr[sl, :] + bias
        m = jnp.max(s, axis=-1, keepdims=True)
        p_scr[sl, :] = jnp.exp2(s - m).astype(bf16)
        if with_sinks:
            if tq % rb == 0:
                sink = sink_ref[(r * rb) // tq]
            else:
                head = jnp.right_shift(r * rb + lax.broadcasted_iota(jnp.int32, (rb, 1), 0), _log2(tq))
                sink = jnp.zeros((rb, 1), f32)
                for h in range(N_HEADS):
                    sink = jnp.where(head == h, sink_ref[h], sink)
            d_scr[sl, :] = jnp.broadcast_to(jnp.exp2(sink * log2e - m), (rb, LANES))

    p_all = p_scr[...]
    acc = jnp.zeros((rows, LANES), f32)
    for vp, off, w, fl in zip(v_parts, offs, widths, flipped):
        acc = acc + (_dot_nt(p_all[:, off:off + w], vp) if fl else _dot(p_all[:, off:off + w], vp))
    den = _dot(p_all, jnp.ones((ktot, LANES), bf16))
    if with_sinks:
        den = den + d_scr[...]
    o_ref[...] = _unstack(acc / den, tq)
    hist_ok = True if hist_always else (i > 0)

    if with_conv:
        cu = cu_ref[...]
        prev = jnp.where(hist_ok, cup_ref[...], 0.0)
        r = lax.broadcasted_iota(jnp.int32, cu.shape, 0)
        p1 = prev[SUBLANES - 1:SUBLANES, :]
        p2 = prev[SUBLANES - 2:SUBLANES - 1, :]
        cu_m1 = jnp.where(r == 0, p1, pltpu.roll(cu, 1, 0))
        cu_m2 = jnp.where(r == 0, p2, jnp.where(r == 1, p1, pltpu.roll(cu, 2, 0)))
        cw = cw_ref[...]
        oc_ref[...] = gb_ref[...] * (cu_m2 * cw[0:1, :] + cu_m1 * cw[1:2, :] + cu * cw[2:3, :])


def _window_attn(q, k, v, k_hist, v_hist, nb, nt, tq, window, conv=None, sinks=None):
    n = q.shape[0]
    hq = q.shape[1]
    per_seq = nt * tq
    cur = lambda b, i: (b * nt + i, 0)
    if k_hist is None:
        hist_always = False
        hb = min(window, tq)
        nh = window // hb
        assert tq % hb == 0 and window % hb == 0

        def hist_map(j):
            return lambda b, i: (b * (per_seq // hb) + jnp.maximum(i * (tq // hb) - (nh - j), 0), 0)

        k_hist, v_hist = k, v
    else:
        hist_always = True
        hb, nh = window, 1
        hist_map = lambda j: (lambda b, i: (b, 0))
    hist_block = (LANES, hb) if hist_always else (hb, LANES)
    tk = max(tq, LANES)
    t_col = jnp.arange(tq, dtype=jnp.int32)[:, None]
    rel_h = jnp.arange(nh * hb, dtype=jnp.int32)[None, :] - nh * hb
    c_own = jnp.arange(tk, dtype=jnp.int32)[None, :]
    ok = jnp.concatenate([rel_h >= t_col - window,
                          (c_own <= t_col) & (c_own >= t_col - window) & (c_own < tq)], axis=1)
    band = jnp.where(ok, 0.0, NEG_INF).astype(f32)
    ktot = nh * hb + tk
    rows = N_HEADS * tq
    args = [q, k, v] + [k_hist] * nh + [v_hist] * nh + [band]
    specs = ([pl.BlockSpec((tq, hq), cur), pl.BlockSpec((tq, LANES), cur), pl.BlockSpec((tq, LANES), cur)]
             + [pl.BlockSpec(hist_block, hist_map(j)) for j in range(nh)] * 2
             + [pl.BlockSpec((tq, ktot), lambda b, i: (0, 0))])
    out_shape = [jax.ShapeDtypeStruct((n, hq), f32)]
    out_specs = [pl.BlockSpec((tq, hq), cur)]
    if conv is not None:
        gb, cu, cu_prev, cw = conv
        cdim = gb.shape[1]
        if cu_prev is None:
            prev = lambda b, i: (jnp.maximum((b * per_seq + i * tq) // SUBLANES - 1, 0), 0)
            cu_prev = cu
        else:
            prev = lambda b, i: (b, 0)
        args += [gb, cu, cu_prev, cw]
        specs += [pl.BlockSpec((tq, cdim), cur), pl.BlockSpec((tq, cdim), cur),
                  pl.BlockSpec((SUBLANES, cdim), prev), pl.BlockSpec(cw.shape, lambda b, i: (0, 0))]
        out_shape.append(jax.ShapeDtypeStruct((n, cdim), f32))
        out_specs.append(pl.BlockSpec((tq, cdim), cur))
    if sinks is not None:
        args.append(sinks)
        specs.append(pl.BlockSpec(memory_space=pltpu.SMEM))
    return pl.pallas_call(
        functools.partial(_win_body, tq, window, hb, nh, hist_always, conv is not None, sinks is not None),
        grid=(nb, nt),
        in_specs=specs,
        out_specs=out_specs,
        out_shape=out_shape,
        scratch_shapes=[pltpu.VMEM((rows, ktot), f32), pltpu.VMEM((rows, ktot), bf16),
                        pltpu.VMEM((tq, ktot), f32), pltpu.VMEM((rows, LANES), f32)],
        compiler_params=_params(2),
        name="window_attn",
    )(*args)


def _cmp_weights(w1, b1, w2):
    n_half = NSA_CMP_LEN // NSA_CMP_STRIDE
    hid = w1.shape[2]
    w1h = w1.reshape(n_half, NSA_CMP_STRIDE, HEAD_DIM, hid)
    big = jnp.zeros((NSA_CMP_STRIDE, N_KV * HEAD_DIM, n_half * N_KV * hid), f32)
    for hh in range(n_half):
        for g in range(N_KV):
            big = big.at[:, g * HEAD_DIM:(g + 1) * HEAD_DIM,
                         hh * N_KV * hid + g * hid:hh * N_KV * hid + (g + 1) * hid].set(w1h[hh])
    w2bd = jnp.zeros((N_KV * hid, N_KV * HEAD_DIM), f32)
    for g in range(N_KV):
        w2bd = w2bd.at[g * hid:(g + 1) * hid, g * HEAD_DIM:(g + 1) * HEAD_DIM].set(w2)
    return big.astype(bf16), jnp.tile(b1, N_KV).reshape(1, N_KV * hid), w2bd.astype(bf16)


def _cmp_finish(hp, n_tok, b1, w2bd):
    rows = hp.shape[0]
    hp1 = pltpu.roll(hp[:, LANES:], rows - 1, 0)
    hid = jax.nn.gelu(hp[:, :LANES] + hp1 + b1)
    return _dot(hid.astype(bf16), w2bd)[:n_tok]


def _cmp_prompt_body(n_chunk, x_ref, w1_ref, b1_ref, w2_ref, o_ref):
    hp = jnp.zeros((n_chunk, 2 * LANES), f32)
    for s in range(NSA_CMP_STRIDE):
        hp = hp + _dot(x_ref[pl.ds(s, n_chunk, stride=NSA_CMP_STRIDE), :].astype(bf16), w1_ref[s])
    r = lax.broadcasted_iota(jnp.int32, (n_chunk, LANES), 0)
    hp1 = jnp.where(r == n_chunk - 1, 0.0, pltpu.roll(hp[:, LANES:], n_chunk - 1, 0))
    hid = jax.nn.gelu(hp[:, :LANES] + hp1 + b1_ref[...])
    o_ref[...] = _dot(hid.astype(bf16), w2_ref[...])


def _compress_prompt(x, nb, t, wts):
    w1, b1, w2 = wts
    n_chunk = t // NSA_CMP_STRIDE
    c3 = lambda b: (0, 0, 0)
    c2 = lambda b: (0, 0)
    return pl.pallas_call(
        functools.partial(_cmp_prompt_body, n_chunk),
        grid=(nb,),
        in_specs=[pl.BlockSpec((t, LANES), lambda b: (b, 0)), pl.BlockSpec(w1.shape, c3),
                  pl.BlockSpec(b1.shape, c2), pl.BlockSpec(w2.shape, c2)],
        out_specs=pl.BlockSpec((n_chunk, LANES), lambda b: (b, 0)),
        out_shape=jax.ShapeDtypeStruct((nb * n_chunk, LANES), f32),
        compiler_params=_params(1),
        name="compress_prompt",
    )(x, w1, b1, w2)


def _paged_scratch(n_pools, pps, page):
    return ([pltpu.VMEM((2, pps, LANES, page), f32) for _ in range(n_pools)]
            + [pltpu.SemaphoreType.DMA((n_pools, 2))])


def _page_copy(pool, buf, sem, k, slot, src_page, i):
    return pltpu.make_async_copy(pool.at[src_page], buf.at[slot, i], sem.at[k, slot])


def _paged_step(pools, bufs, sem, tab_ref, pps):
    nc = pl.num_programs(1)
    s = pl.program_id(0) * nc + pl.program_id(1)
    total = pl.num_programs(0) * nc
    slot = lax.rem(s, 2)

    def fetch(step, sl):
        def start(i, carry):
            src_page = tab_ref[step * pps + i]
            for k, (pool, buf) in enumerate(zip(pools, bufs)):
                _page_copy(pool, buf, sem, k, sl, src_page, i).start()
            return carry
        lax.fori_loop(0, pps, start, 0)

    @pl.when(s == 0)
    def _():
        fetch(0, 0)

    @pl.when(s + 1 < total)
    def _():
        fetch(s + 1, 1 - slot)

    for k, (pool, buf) in enumerate(zip(pools, bufs)):
        pltpu.make_async_copy(pool.at[pl.ds(0, pps)], buf.at[slot], sem.at[k, slot]).wait()
    return slot


def _cmp_sample_body(pps, nc, page, n_past_chunk, tab_ref, pool_ref, xn_ref, w1_ref, b1_ref, w2_ref, o_ref,
                     hp_scr, buf, sem):
    c = pl.program_id(1)
    slot = _paged_step([pool_ref], [buf], sem, tab_ref, pps)
    pages = [buf.at[slot, i] for i in range(pps)]
    cpp = page // NSA_CMP_STRIDE
    out_r = lax.broadcasted_iota(jnp.int32, (page, page), 0)
    src_r = lax.broadcasted_iota(jnp.int32, (page, page), 1)
    src_of = (out_r & (cpp - 1)) * NSA_CMP_STRIDE + jnp.right_shift(out_r, _log2(cpp))
    perm = jnp.where(src_r == src_of, 1.0, 0.0).astype(bf16)
    regrouped = [_dot_nt(perm, p[...].astype(bf16)) for p in pages]
    flat = jnp.concatenate([jnp.concatenate([x[s * cpp:(s + 1) * cpp] for x in regrouped], axis=0)
                            for s in range(NSA_CMP_STRIDE)], axis=1)
    hp = _dot(flat.astype(bf16), w1_ref[...].reshape(NSA_CMP_STRIDE * LANES, 2 * LANES))
    hp_scr[pl.ds(pl.multiple_of(c * (pps * cpp), SUBLANES), pps * cpp), :] = hp

    @pl.when(c == nc - 1)
    def _():
        xn = xn_ref[...].astype(bf16)
        r = lax.broadcasted_iota(jnp.int32, (NSA_CMP_STRIDE, 2 * LANES), 0)
        new = jnp.zeros((NSA_CMP_STRIDE, 2 * LANES), f32)
        for s in range(NSA_CMP_STRIDE):
            new = new + jnp.where(r == s, _dot(xn, w1_ref[s]), 0.0)
        new = jnp.sum(new, axis=0, keepdims=True)
        r8 = lax.broadcasted_iota(jnp.int32, (SUBLANES, 2 * LANES), 0)
        hp_scr[n_past_chunk:n_past_chunk + SUBLANES, :] = jnp.where(r8 == 0, new, 0.0)
        o_ref[...] = _cmp_finish(hp_scr[...], n_past_chunk, b1_ref[...], w2_ref[...])


def _compress_sample(pool, table, x_new, nb, t_new, wts):
    w1, b1, w2 = wts
    page = pool.shape[2]
    n_pages = table.shape[0] // nb
    pps = min(PAGES_PER_STEP, n_pages)
    nc = n_pages // pps
    n_past_chunk = n_pages * page // NSA_CMP_STRIDE
    assert t_new <= NSA_CMP_STRIDE
    xn = jnp.pad(x_new.reshape(nb, t_new, LANES), ((0, 0), (0, NSA_CMP_STRIDE - t_new), (0, 0)))
    xn = xn.reshape(nb * NSA_CMP_STRIDE, LANES)
    c3 = lambda b, c, tab: (0, 0, 0)
    c2 = lambda b, c, tab: (0, 0)
    grid_spec = pltpu.PrefetchScalarGridSpec(
        num_scalar_prefetch=1,
        grid=(nb, nc),
        in_specs=[pl.BlockSpec(memory_space=pl.ANY),
                  pl.BlockSpec((NSA_CMP_STRIDE, LANES), lambda b, c, tab: (b, 0)), pl.BlockSpec(w1.shape, c3),
                  pl.BlockSpec(b1.shape, c2), pl.BlockSpec(w2.shape, c2)],
        out_specs=pl.BlockSpec((n_past_chunk, LANES), lambda b, c, tab: (b, 0)),
        scratch_shapes=[pltpu.VMEM((n_past_chunk + SUBLANES, 2 * LANES), f32)] + _paged_scratch(1, pps, page),
    )
    return pl.pallas_call(
        functools.partial(_cmp_sample_body, pps, nc, page, n_past_chunk),
        grid_spec=grid_spec,
        out_shape=jax.ShapeDtypeStruct((nb * n_past_chunk, LANES), f32),
        compiler_params=_params(2),
        name="compress_sample",
    )(table, pool, xn, w1, b1, w2)


def _means_prompt_body(n_blk, nbp, k_ref, o_ref):
    t = k_ref.shape[0]
    sums = jnp.sum(k_ref[...].reshape(n_blk, t // n_blk, LANES), axis=1)
    o_ref[...] = _pad_rows(sums * (1.0 / MOBA_BLOCK), nbp).T


def _means_prompt(k, nb, t, nbp):
    n_blk = t // MOBA_BLOCK
    return pl.pallas_call(
        functools.partial(_means_prompt_body, n_blk, nbp),
        grid=(nb,),
        in_specs=[pl.BlockSpec((t, LANES), lambda b: (b, 0))],
        out_specs=pl.BlockSpec((LANES, nbp), lambda b: (b, 0)),
        out_shape=jax.ShapeDtypeStruct((nb * LANES, nbp), f32),
        compiler_params=_params(1),
        name="moba_means_prompt",
    )(k)


def _means_sample_body(pps, nc, n_past_blk, nbp, tab_ref, pool_ref, kn_ref, o_ref, buf, sem):
    c = pl.program_id(1)
    slot = _paged_step([pool_ref], [buf], sem, tab_ref, pps)
    pages = [buf.at[slot, i] for i in range(pps)]
    per = pps // 2

    lane = lax.broadcasted_iota(jnp.int32, (LANES, nbp), 1)
    @pl.when(c == 0)
    def _():
        o_ref[...] = jnp.zeros((LANES, nbp), f32)

    acc = o_ref[...]
    for i in range(per):
        col = jnp.sum(pages[2 * i][...] + pages[2 * i + 1][...], axis=1, keepdims=True) * (1.0 / MOBA_BLOCK)
        acc = jnp.where(lane == c * per + i, col, acc)
    new = jnp.sum(kn_ref[...], axis=1, keepdims=True) * (1.0 / MOBA_BLOCK)
    acc = jnp.where((lane == n_past_blk) & (c == nc - 1), new, acc)
    o_ref[...] = acc


def _means_sample(pool, table, k_new_t, nb, t_new, nbp):
    page = pool.shape[2]
    n_pages = table.shape[0] // nb
    pps = min(PAGES_PER_STEP, n_pages)
    nc = n_pages // pps
    assert MOBA_BLOCK == 2 * page and t_new <= MOBA_BLOCK and pps % 2 == 0
    n_past_blk = n_pages * page // MOBA_BLOCK
    grid_spec = pltpu.PrefetchScalarGridSpec(
        num_scalar_prefetch=1,
        grid=(nb, nc),
        in_specs=[pl.BlockSpec(memory_space=pl.ANY), pl.BlockSpec((LANES, t_new), lambda b, c, tab: (b, 0))],
        out_specs=pl.BlockSpec((LANES, nbp), lambda b, c, tab: (b, 0)),
        scratch_shapes=_paged_scratch(1, pps, page),
    )
    return pl.pallas_call(
        functools.partial(_means_sample_body, pps, nc, n_past_blk, nbp),
        grid_spec=grid_spec,
        out_shape=jax.ShapeDtypeStruct((nb * LANES, nbp), f32),
        compiler_params=_params(2),
        name="moba_means_sample",
    )(table, pool, k_new_t)


def _topk_cols(score, k):
    n = score.shape[0]
    ridx = lax.broadcasted_iota(jnp.int32, score.shape, 0)
    sel = jnp.zeros(score.shape, f32)
    work = score
    for _ in range(k):
        mx = jnp.max(work, axis=0, keepdims=True)
        cand = (work == mx) & (mx > -jnp.inf)
        first = jnp.min(jnp.where(cand, ridx, n), axis=0, keepdims=True)
        pick = ridx == first
        sel = jnp.where(pick, 1.0, sel)
        work = jnp.where(pick, -jnp.inf, work)
    return sel


def _moba_sel_body(tq, p0, nbp, q_ref, mean_ref, o_ref):
    i = pl.program_id(1)
    rows = N_HEADS * tq
    cols = max(rows, LANES)
    qbd = _pad_rows(_build_qbd(q_ref[...], 1.0).astype(f32), cols)
    q_hi = qbd.astype(bf16)
    q_lo = (qbd - q_hi.astype(f32)).astype(bf16)
    mean = mean_ref[...]
    m_hi = mean.astype(bf16)
    m_lo = (mean - m_hi.astype(f32)).astype(bf16)
    s = (_dot(q_hi, m_hi) + _dot(q_lo, m_hi) + _dot(q_hi, m_lo)).T
    n_idx = lax.broadcasted_iota(jnp.int32, (nbp, cols), 0)
    t = lax.broadcasted_iota(jnp.int32, (nbp, cols), 1) & (tq - 1)
    cur = jnp.right_shift(p0 + i * tq + t, _log2(MOBA_BLOCK))
    sel = _topk_cols(jnp.where(n_idx < cur, s, -jnp.inf), MOBA_TOPK)
    sel = jnp.where(n_idx == cur, 1.0, sel)
    sel_t = sel.T
    for h in range(N_HEADS):
        o_ref[h] = sel_t[h * tq:(h + 1) * tq]


def _moba_select(q, means, nb, nt, tq, p0, nbp):
    n, hq = q.shape
    return pl.pallas_call(
        functools.partial(_moba_sel_body, tq, p0, nbp),
        grid=(nb, nt),
        in_specs=[pl.BlockSpec((tq, hq), lambda b, i: (b * nt + i, 0)),
                  pl.BlockSpec((LANES, nbp), lambda b, i: (b, 0))],
        out_specs=pl.BlockSpec((N_HEADS, tq, nbp), lambda b, i: (0, b * nt + i, 0)),
        out_shape=jax.ShapeDtypeStruct((N_HEADS, n, nbp), f32),
        compiler_params=_params(2),
        name="moba_select",
    )(q, means)


def _cmp_sel_body(tq, p0, n_ck, n_cmp, nsp, q_ref, ck_ref, cv_ref, o_ref, m_ref):
    i = pl.program_id(1)
    rows = N_HEADS * tq
    per = N_HEADS // N_KV
    qbd = _build_qbd(q_ref[...], HEAD_DIM ** -0.5)
    n_idx = lax.broadcasted_iota(jnp.int32, (rows, n_ck), 1)
    pos = p0 + i * tq + _row_token(rows, tq, n_ck)
    valid = (n_idx * NSA_CMP_STRIDE + (NSA_CMP_LEN - 1) <= pos) & (n_idx < n_cmp)
    s = jnp.where(valid, _dot_nt(qbd, ck_ref[...].astype(bf16)), NEG_INF)
    mx = jnp.max(s, axis=-1, keepdims=True)
    e = jnp.where(valid, jnp.exp(s - mx), 0.0)
    den = jnp.sum(e, axis=-1, keepdims=True)
    p = e * jnp.where(den > 0.0, 1.0 / den, 0.0)
    o_ref[...] = _unstack(_dot(p.astype(bf16), cv_ref[...].astype(bf16)), tq)

    cols = max(tq, LANES)
    j_idx = lax.broadcasted_iota(jnp.int32, (nsp, n_ck), 0)
    n_of = lax.broadcasted_iota(jnp.int32, (nsp, n_ck), 1)
    ratio = NSA_SEL_BLOCK // NSA_CMP_STRIDE
    m_span = NSA_CMP_LEN // NSA_CMP_STRIDE
    lo = ratio * j_idx - (m_span - 1)
    a_t = jnp.where((n_of >= lo) & (n_of <= lo + ratio + m_span - 2), 1.0, 0.0).astype(bf16)
    j_col = lax.broadcasted_iota(jnp.int32, (nsp, N_KV * cols), 0)
    t_col = lax.broadcasted_iota(jnp.int32, (nsp, N_KV * cols), 1) & (cols - 1)
    j_cur = jnp.right_shift(p0 + i * tq + t_col, _log2(NSA_SEL_BLOCK))
    forced = (j_col == 0) | (j_col == j_cur) | (j_col == j_cur - 1)
    p_slc = []
    for g in range(N_KV):
        imp = p[(g * per) * tq:(g * per + 1) * tq]
        for r in range(1, per):
            imp = imp + p[(g * per + r) * tq:(g * per + r + 1) * tq]
        imp = _pad_rows(imp, cols)
        hi, mid, lo3 = _split3(imp)
        p_slc.append(_dot_nt(a_t, hi) + _dot_nt(a_t, mid) + _dot_nt(a_t, lo3))
    score = jnp.where(forced, FORCE_SCORE, jnp.where(j_col <= j_cur, jnp.concatenate(p_slc, axis=1), -jnp.inf))
    sel = _topk_cols(score, NSA_TOPN)
    for g in range(N_KV):
        sel_t = sel[:, g * cols:(g + 1) * cols].T[:tq]
        for r in range(per):
            m_ref[g * per + r] = sel_t


def _cmp_select(q, ck, cv, nb, nt, tq, p0, n_ck, n_cmp, nsp):
    n, hq = q.shape
    cur = lambda b, i: (b * nt + i, 0)
    return pl.pallas_call(
        functools.partial(_cmp_sel_body, tq, p0, n_ck, n_cmp, nsp),
        grid=(nb, nt),
        in_specs=[pl.BlockSpec((tq, hq), cur), pl.BlockSpec((n_ck, LANES), lambda b, i: (b, 0)),
                  pl.BlockSpec((n_ck, LANES), lambda b, i: (b, 0))],
        out_specs=[pl.BlockSpec((tq, hq), cur),
                   pl.BlockSpec((N_HEADS, tq, nsp), lambda b, i: (0, b * nt + i, 0))],
        out_shape=[jax.ShapeDtypeStruct((n, hq), f32), jax.ShapeDtypeStruct((N_HEADS, n, nsp), f32)],
        compiler_params=_params(2),
        name="cmp_select",
    )(q, ck, cv)


def _blk_prompt_body(tq, blk_shift, q_ref, k_ref, v_ref, mask_ref, o_ref,
                     kaug_scr, vaug_scr, qaug_scr, s_scr, p_scr, a_scr, m_scr, acc_scr):
    i = pl.program_id(1)
    rows = N_HEADS * tq
    nbp = mask_ref.shape[2]
    t_len = k_ref.shape[0]

    @pl.when(i == 0)
    def _():
        key = lax.broadcasted_iota(jnp.int32, (nbp, t_len), 1)
        blk = lax.broadcasted_iota(jnp.int32, (nbp, t_len), 0)
        kaug_scr[:LANES, :] = k_ref[...].T.astype(bf16)
        kaug_scr[LANES:, :] = jnp.where(blk == jnp.right_shift(key, blk_shift), NEG_INF, 0.0).astype(bf16)
        ones = jnp.where(lax.broadcasted_iota(jnp.int32, (t_len, LANES), 1) == 0, 1.0, 0.0)
        vaug_scr[:, :LANES] = v_ref[...].astype(bf16)
        vaug_scr[:, LANES:] = ones.astype(bf16)

    unselected = 1.0 - mask_ref[...].reshape(rows, nbp)
    qbd = _build_qbd(q_ref[...], HEAD_DIM ** -0.5 * math.log2(math.e))
    qaug_scr[...] = jnp.concatenate([qbd, unselected.astype(bf16)], axis=1)
    m_scr[...] = jnp.full((rows, LANES), NEG_INF, f32)
    acc_scr[...] = jnp.zeros((rows, 2 * LANES), f32)

    def tile(j, causal):
        st = pl.multiple_of(j * tq, tq)
        s_scr[...] = _dot(qaug_scr[...], kaug_scr[:, pl.ds(st, tq)])
        for r in range(rows // SOFTMAX_ROWS):
            sl = slice(r * SOFTMAX_ROWS, (r + 1) * SOFTMAX_ROWS)
            s = s_scr[sl, :]
            if causal:
                t_row = (r * SOFTMAX_ROWS + lax.broadcasted_iota(jnp.int32, (SOFTMAX_ROWS, tq), 0)) & (tq - 1)
                s = jnp.where(lax.broadcasted_iota(jnp.int32, (SOFTMAX_ROWS, tq), 1) <= t_row, s, NEG_INF)
            m_old = m_scr[sl, :]
            m_new = jnp.maximum(m_old, jnp.max(s, axis=-1, keepdims=True))
            p_scr[sl, :] = jnp.exp2(s - jnp.concatenate([m_new] * (tq // LANES), axis=1)).astype(bf16)
            a_scr[sl, :] = jnp.exp2(m_old - m_new)
            m_scr[sl, :] = m_new
        a = a_scr[...]
        acc_scr[...] = jnp.concatenate([a, a], axis=1) * acc_scr[...] + _dot(p_scr[...], vaug_scr[pl.ds(st, tq), :])

    tile(i, True)
    lax.fori_loop(0, i, lambda j, carry: (tile(j, False), carry)[1], 0)
    acc = acc_scr[...]
    o_ref[...] = _unstack(acc[:, :LANES] / acc[:, LANES:LANES + 1], tq)


def _block_attn_prompt(q, k, v, mask, nb, nt, tq, blk):
    n, hq = q.shape
    t = nt * tq
    nbp = mask.shape[2]
    assert nbp == LANES
    rows = N_HEADS * tq
    cur = lambda b, i: (b * nt + i, 0)
    seq = lambda b, i: (b, 0)
    return pl.pallas_call(
        functools.partial(_blk_prompt_body, tq, _log2(blk)),
        grid=(nb, nt),
        in_specs=[pl.BlockSpec((tq, hq), cur), pl.BlockSpec((t, LANES), seq), pl.BlockSpec((t, LANES), seq),
                  pl.BlockSpec((N_HEADS, tq, nbp), lambda b, i: (0, b * nt + i, 0))],
        out_specs=pl.BlockSpec((tq, hq), cur),
        out_shape=jax.ShapeDtypeStruct((n, hq), f32),
        scratch_shapes=[pltpu.VMEM((2 * LANES, t), bf16), pltpu.VMEM((t, 2 * LANES), bf16),
                        pltpu.VMEM((rows, 2 * LANES), bf16), pltpu.VMEM((rows, tq), f32),
                        pltpu.VMEM((rows, tq), bf16), pltpu.VMEM((rows, LANES), f32),
                        pltpu.VMEM((rows, LANES), f32), pltpu.VMEM((rows, 2 * LANES), f32)],
        compiler_params=_params(2),
        name="block_attn_prompt",
    )(q, k, v, mask)


def _blk_sample_body(pps, nc, tq, blk_shift, tab_ref, kpool_ref, vpool_ref, q_ref, kn_ref, vn_ref, mpast_ref,
                     mnew_ref, aux_ref, o_ref, qbd_scr, m_scr, acc_scr, kbuf, vbuf, sem):
    c = pl.program_id(1)
    rows = N_HEADS * tq
    slot = _paged_step([kpool_ref, vpool_ref], [kbuf, vbuf], sem, tab_ref, pps)
    kp = [kbuf.at[slot, i] for i in range(pps)]
    vp = [vbuf.at[slot, i] for i in range(pps)]

    @pl.when(c == 0)
    def _():
        qbd0 = _build_qbd(q_ref[...], HEAD_DIM ** -0.5)
        qbd_scr[...] = qbd0
        tn = max(tq, LANES)
        idx = lax.broadcasted_iota(jnp.int32, (1, tn), 1)
        allowed = (_expand_mask(mnew_ref[0].astype(bf16), idx, blk_shift) & (idx <= _row_token(rows, tq, tn))
                   & (idx < tq))
        s0 = jnp.where(allowed, _dot_nt(qbd0, _pad_rows(kn_ref[...], tn).astype(bf16)), NEG_INF)
        m0 = jnp.max(s0, axis=-1, keepdims=True)
        ones = jnp.where(lax.broadcasted_iota(jnp.int32, (tn, LANES), 1) == 0, 1.0, 0.0)
        vaug_n = jnp.concatenate([_pad_rows(vn_ref[...], tn).astype(bf16), ones.astype(bf16)], axis=1)
        m_scr[...] = m0
        acc_scr[...] = _dot(jnp.exp(s0 - m0).astype(bf16), vaug_n)

    unselected = (1.0 - mpast_ref[0, 0]).astype(bf16)
    k_t = jnp.concatenate([p[...] for p in kp], axis=1).astype(bf16)
    s = _dot(qbd_scr[...], k_t) + _dot(unselected, aux_ref[0:LANES, :])
    m_old = m_scr[...]
    m_new = jnp.maximum(m_old, jnp.max(s, axis=-1, keepdims=True))
    p = jnp.exp(s - m_new).astype(bf16)
    v_t = jnp.concatenate([p_[...] for p_ in vp], axis=1).astype(bf16)
    pv = jnp.concatenate([_dot_nt(p, v_t), _dot_nt(p, aux_ref[LANES:, :])], axis=1)
    acc = jnp.exp(m_old - m_new) * acc_scr[...] + pv
    m_scr[...] = m_new
    acc_scr[...] = acc

    @pl.when(c == nc - 1)
    def _():
        o_ref[...] = _unstack(acc[:, :LANES] / acc[:, LANES:LANES + 1], tq)


def _block_attn_sample(q, pool_k, pool_v, table, k_new, v_new, mask, nb, tq, blk):
    n, hq = q.shape
    page = pool_k.shape[2]
    n_pages = table.shape[0] // nb
    pps = min(PAGES_PER_STEP, n_pages)
    nc = n_pages // pps
    nbp = mask.shape[2]
    rows = N_HEADS * tq
    tk = pps * page
    bpc = tk // blk
    n_past_blk = nc * bpc
    assert bpc <= LANES and nbp - n_past_blk <= LANES
    m4 = mask.reshape(N_HEADS, nb, tq, nbp)
    mpast = m4[..., :n_past_blk].reshape(N_HEADS, nb, tq, nc, bpc)
    mpast = jnp.transpose(mpast, (1, 3, 0, 2, 4)).reshape(nb, nc, rows, bpc)
    mpast = jnp.pad(mpast, ((0, 0), (0, 0), (0, 0), (0, LANES - bpc)))
    mnew = jnp.transpose(m4[..., n_past_blk:], (1, 0, 2, 3)).reshape(nb, rows, nbp - n_past_blk)
    mnew = jnp.pad(mnew, ((0, 0), (0, 0), (0, LANES - (nbp - n_past_blk))))
    col_blk = jnp.arange(tk, dtype=jnp.int32)[None, :] // blk
    row = jnp.arange(LANES, dtype=jnp.int32)[:, None]
    aux = jnp.concatenate([jnp.where(row == col_blk, NEG_INF, 0.0),
                           jnp.where(row == 0, 1.0, 0.0) * jnp.ones((1, tk), f32)], axis=0).astype(bf16)
    cur = lambda b, c, tab: (b, 0)
    grid_spec = pltpu.PrefetchScalarGridSpec(
        num_scalar_prefetch=1,
        grid=(nb, nc),
        in_specs=[pl.BlockSpec(memory_space=pl.ANY), pl.BlockSpec(memory_space=pl.ANY),
                  pl.BlockSpec((tq, hq), cur), pl.BlockSpec((tq, LANES), cur), pl.BlockSpec((tq, LANES), cur),
                  pl.BlockSpec((1, 1, rows, LANES), lambda b, c, tab: (b, c, 0, 0)),
                  pl.BlockSpec((1, rows, LANES), lambda b, c, tab: (b, 0, 0)),
                  pl.BlockSpec((2 * LANES, tk), lambda b, c, tab: (0, 0))],
        out_specs=pl.BlockSpec((tq, hq), cur),
        scratch_shapes=[pltpu.VMEM((rows, LANES), bf16), pltpu.VMEM((rows, 1), f32),
                        pltpu.VMEM((rows, 2 * LANES), f32)] + _paged_scratch(2, pps, page),
    )
    return pl.pallas_call(
        functools.partial(_blk_sample_body, pps, nc, tq, _log2(blk)),
        grid_spec=grid_spec,
        out_shape=jax.ShapeDtypeStruct((n, hq), f32),
        compiler_params=_params(2),
        name="block_attn_sample",
    )(table, pool_k, pool_v, q, k_new, v_new, mpast, mnew, aux)


def _round_up(x, m):
    return -(-x // m) * m


def _rows_last(a):
    n, rows = a.shape[:2]
    return jnp.transpose(a, (0, 2, 3, 1)).reshape(n * N_KV * HEAD_DIM, rows)


def _forward(x, nb, t, p0, past, prm):
    n = nb * t
    prompt = past is None
    tm = min(FFN_ROWS, n)
    if prompt:
        pos = jnp.arange(t, dtype=jnp.int32)
        n_tab = t // tm
    else:
        pos = p0 + jnp.arange(tm, dtype=jnp.int32) % t
        n_tab = 1
    tables = _rope_tables(pos)

    x = _ffn(x, prm["norm_ffn_a"][0], *prm["ffn_a"][0])
    cdim = prm["l0_conv_w"].shape[1]
    hq, hk = N_HEADS * HEAD_DIM, N_KV * HEAD_DIM
    gb, cu, q, k, v = _proj(functools.partial(_proj_ab_body, cdim), x, prm["norm_mix"][0], prm["l0_w_in"],
                            tables, n_tab, (cdim, cdim, hq, hk, hk))
    if prompt:
        tq = min(ATTN_ROWS, t)
        y_attn, y_conv = _window_attn(q, k, v, None, None, nb, t // tq, tq, SWA_WINDOW,
                                      conv=(gb, cu, None, prm["l0_conv_w"]), sinks=prm["l0_sinks"])
    else:
        conv_buf = jnp.pad(past["conv"], ((0, 0), (SUBLANES - (CONV_WIDTH - 1), 0), (0, 0)))
        y_attn, y_conv = _window_attn(q, k, v, _rows_last(past["swa_k"]), _rows_last(past["swa_v"]),
                                      nb, 1, t, SWA_WINDOW,
                                      conv=(gb, cu, conv_buf.reshape(nb * SUBLANES, cdim), prm["l0_conv_w"]),
                                      sinks=prm["l0_sinks"])
    x = _ffn(x, prm["norm_ffn_b"][0], *prm["ffn_b"][0], mix=(y_conv, y_attn, prm["l0_w_out"]))
    states0 = (cu, k, v)

    x = _ffn(x, prm["norm_ffn_a"][1], *prm["ffn_a"][1])
    qm, km, vm, qn, kc, vc, ks, vs, kw, vw, gates = _proj(
        _proj_cd_body, x, prm["norm_mix"][1], prm["l1_w_in"], tables, n_tab,
        (hq, hk, hk, hq, hk, hk, hk, hk, hk, hk, LANES))
    if prompt:
        tq = min(ATTN_ROWS, t)
        nt = t // tq
        n_blk = t // MOBA_BLOCK
        nbp = _round_up(n_blk + 1, LANES)
        means = _means_prompt(km, nb, t, nbp)
        moba_mask = _moba_select(qm, means, nb, nt, tq, 0, nbp)
        o_moba = _block_attn_prompt(qm, km, vm, moba_mask, nb, nt, tq, MOBA_BLOCK)
        ck = _compress_prompt(kc, nb, t, prm["cmp_k"])
        cv = _compress_prompt(vc, nb, t, prm["cmp_v"])
        n_ck = t // NSA_CMP_STRIDE
        nsp = _round_up(-(-t // NSA_SEL_BLOCK), LANES)
        o_cmp, sel_mask = _cmp_select(qn, ck, cv, nb, nt, tq, 0, n_ck, n_ck - 1, nsp)
        o_sel = _block_attn_prompt(qn, ks, vs, sel_mask, nb, nt, tq, NSA_SEL_BLOCK)
        (o_win,) = _window_attn(qn, kw, vw, None, None, nb, nt, tq, NSA_WINDOW)
        half = HEAD_DIM // 2
        inv_freq = ROPE_THETA ** (-jnp.arange(half, dtype=f32) * 2.0 / HEAD_DIM)
        ang_t = inv_freq[:, None] * pos.astype(f32)[None, :]
        rows_last = _proj_t(x, prm["norm_mix"][1], prm["l1_w_in_t"], jnp.cos(ang_t), jnp.sin(ang_t), nb, t,
                            (True, False, False, False, True, False))
        km, vm, kc, vc, ks, vs = [jnp.transpose(a.reshape(nb, N_KV, HEAD_DIM, t), (0, 3, 1, 2)) for a in rows_last]
    else:
        table = past["table"]
        n_past = p0
        n_blk = n_past // MOBA_BLOCK
        nbp = _round_up(n_blk + 1, LANES)
        km_t = jnp.transpose(km.reshape(nb, t, hk), (0, 2, 1)).reshape(nb * hk, t)
        means = _means_sample(past["moba_k"], table, km_t, nb, t, nbp)
        moba_mask = _moba_select(qm, means, nb, 1, t, p0, nbp)
        o_moba = _block_attn_sample(qm, past["moba_k"], past["moba_v"], table, km, vm, moba_mask, nb, t,
                                    MOBA_BLOCK)
        ck = _compress_sample(past["cmp_k"], table, kc, nb, t, prm["cmp_k"])
        cv = _compress_sample(past["cmp_v"], table, vc, nb, t, prm["cmp_v"])
        n_ck = n_past // NSA_CMP_STRIDE
        nsp = _round_up(-(-(n_past + t) // NSA_SEL_BLOCK), LANES)
        o_cmp, sel_mask = _cmp_select(qn, ck, cv, nb, 1, t, p0, n_ck, n_ck, nsp)
        o_sel = _block_attn_sample(qn, past["sel_k"], past["sel_v"], table, ks, vs, sel_mask, nb, t,
                                   NSA_SEL_BLOCK)
        (o_win,) = _window_attn(qn, kw, vw, _rows_last(past["win_k"]), _rows_last(past["win_v"]),
                                nb, 1, t, NSA_WINDOW)
    x = _ffn(x, prm["norm_ffn_b"][1], *prm["ffn_b"][1], mix=(o_moba, o_cmp, o_sel, o_win, gates, prm["l1_w_out"]),
             final_g=prm["norm_final"])
    return x, states0 + (km, vm, kc, vc, ks, vs, kw, vw)


def _tail(buf, new, rows):
    full = new if buf is None else jnp.concatenate([buf.astype(new.dtype), new], axis=1)
    return full[:, full.shape[1] - rows:]


def kernel(x_prompt, x_sample, state_l0_conv, cache_l0_swa_k, cache_l0_swa_v, cache_l1_moba_k, cache_l1_moba_v, cache_l1_nsa_cmp_k, cache_l1_nsa_cmp_v, cache_l1_nsa_sel_k, cache_l1_nsa_sel_v, cache_l1_nsa_win_k, cache_l1_nsa_win_v, page_table, norm_ffn_a, ffn_a_gate, ffn_a_up, ffn_a_down, norm_mix, norm_ffn_b, ffn_b_gate, ffn_b_up, ffn_b_down, norm_final, l0_w_in, l0_conv_w, l0_sinks, l0_w_out, l1_w_in, l1_cmp_k_w1, l1_cmp_k_b1, l1_cmp_k_w2, l1_cmp_v_w1, l1_cmp_v_b1, l1_cmp_v_w2, l1_w_out):
    bp, tp, d = x_prompt.shape
    bs, ts, _ = x_sample.shape
    depth = norm_ffn_a.shape[0]
    assert depth == 2 and N_KV * HEAD_DIM == LANES
    n_pool, page = cache_l1_moba_k.shape[:2]
    p0 = page_table.shape[1] * page
    hk = N_KV * HEAD_DIM
    hq = N_HEADS * HEAD_DIM

    w1_pad = _round_up(l1_w_in.shape[1], LANES) - l1_w_in.shape[1]
    prm = {
        "norm_ffn_a": norm_ffn_a, "norm_mix": norm_mix, "norm_ffn_b": norm_ffn_b, "norm_final": norm_final,
        "ffn_a": [(ffn_a_gate[l].astype(bf16), ffn_a_up[l].astype(bf16), ffn_a_down[l].astype(bf16))
                  for l in range(depth)],
        "ffn_b": [(ffn_b_gate[l].astype(bf16), ffn_b_up[l].astype(bf16), ffn_b_down[l].astype(bf16))
                  for l in range(depth)],
        "l0_w_in": l0_w_in.astype(bf16), "l0_conv_w": l0_conv_w, "l0_sinks": l0_sinks,
        "l0_w_out": l0_w_out.astype(bf16),
        "l1_w_in": jnp.pad(l1_w_in, ((0, 0), (0, w1_pad))).astype(bf16), "l1_w_out": l1_w_out.astype(bf16),
        "l1_w_in_t": jnp.concatenate([l1_w_in[:, hq:hq + 2 * hk], l1_w_in[:, 2 * hq + 2 * hk:2 * hq + 6 * hk]],
                                     axis=1).T.astype(bf16),
        "cmp_k": _cmp_weights(l1_cmp_k_w1, l1_cmp_k_b1, l1_cmp_k_w2),
        "cmp_v": _cmp_weights(l1_cmp_v_w1, l1_cmp_v_b1, l1_cmp_v_w2),
    }
    pool = lambda a: _rows_last(a).reshape(n_pool, hk, page)
    past = {
        "conv": state_l0_conv, "swa_k": cache_l0_swa_k, "swa_v": cache_l0_swa_v,
        "moba_k": pool(cache_l1_moba_k), "moba_v": pool(cache_l1_moba_v),
        "cmp_k": pool(cache_l1_nsa_cmp_k), "cmp_v": pool(cache_l1_nsa_cmp_v),
        "sel_k": pool(cache_l1_nsa_sel_k), "sel_v": pool(cache_l1_nsa_sel_v),
        "win_k": cache_l1_nsa_win_k, "win_v": cache_l1_nsa_win_v,
        "table": page_table.reshape(-1),
    }
    y_p, st_p = _forward(x_prompt.reshape(bp * tp, d), bp, tp, 0, None, prm)
    y_s, st_s = _forward(x_sample.reshape(bs * ts, d), bs, ts, p0, past, prm)

    def states(st, nb, t, bufs):
        cu, k0, v0, km, vm, kc, vc, ks, vs, kw, vw = st
        kv = lambda a: a if a.ndim == 4 else a.reshape(nb, t, N_KV, HEAD_DIM)
        conv = _tail(bufs["conv"], cu.reshape(nb, t, -1), CONV_WIDTH - 1)
        swa_rows = min(SWA_WINDOW, t) if bufs["swa_k"] is None else bufs["swa_k"].shape[1]
        win_rows = min(NSA_WINDOW, t) if bufs["win_k"] is None else bufs["win_k"].shape[1]
        return (conv, _tail(bufs["swa_k"], kv(k0), swa_rows), _tail(bufs["swa_v"], kv(v0), swa_rows),
                kv(km), kv(vm), kv(kc), kv(vc), kv(ks), kv(vs),
                _tail(bufs["win_k"], kv(kw), win_rows), _tail(bufs["win_v"], kv(vw), win_rows))

    none = {"conv": None, "swa_k": None, "swa_v": None, "win_k": None, "win_v": None}
    sp = states(st_p, bp, tp, none)
    ss = states(st_s, bs, ts, past)
    conv_p, swa_k_p, swa_v_p, mk_p, mv_p, ck_p, cv_p, sk_p, sv_p, wk_p, wv_p = sp
    conv_s, swa_k_s, swa_v_s, mk_s, mv_s, ck_s, cv_s, sk_s, sv_s, wk_s, wv_s = ss
    return (y_p.reshape(bp, tp, d), y_s.reshape(bs, ts, d), conv_p, conv_s, swa_k_p, swa_v_p, swa_k_s, swa_v_s,
            mk_p, mv_p, mk_s, mv_s, ck_p, cv_p, ck_s, cv_s,
            sk_p, sv_p, sk_s, sv_s, wk_p, wv_p, wk_s, wv_s)
```
